```python
import math
import jax, jax.numpy as jnp
from jax import lax
import numpy as np

D_MODEL = 1024
BATCH = 4
SEQ = 8192
DEPTH = 2

N_META = 16
Q_BLOCK = 128
HEAD_DIM = 64
N_HEADS_A = D_MODEL // (2 * HEAD_DIM)
N_MAPS_A = 2 * N_HEADS_A
N_HEADS_B = D_MODEL // HEAD_DIM
N_BUCKETS = 32
MAX_EXACT = N_BUCKETS // 2
MAX_DISTANCE = 128
N_GROUPS = 4
EXPERTS_PER_GROUP = 8
N_EXPERTS = N_GROUPS * EXPERTS_PER_GROUP
TOP_K_IN_GROUP = 2
D_EXPERT = 512
N_MIXERS = 2
N_LAYERS_A = (DEPTH + 1) // 2
N_LAYERS_B = DEPTH // 2
DN_ALPHA = (2 * DEPTH) ** 0.25
DN_BETA = (8 * DEPTH) ** -0.25
LN_EPS = 1e-5
NEG_INF = -1e30

kernel_name = 'hybrid_diffattn_fox_hmoe_deepnorm'


def layer_norm(x, g, b):
    xf = x.astype(jnp.float32)
    mu = jnp.mean(xf, axis=-1, keepdims=True)
    var = jnp.mean(jnp.square(xf - mu), axis=-1, keepdims=True)
    y = (xf - mu) * lax.rsqrt(var + LN_EPS) * g.astype(jnp.float32) + b.astype(jnp.float32)
    return y.astype(x.dtype)


def rms_norm(x, w):
    xf = x.astype(jnp.float32)
    y = xf * lax.rsqrt(jnp.mean(xf * xf, axis=-1, keepdims=True) + LN_EPS) * w.astype(jnp.float32)
    return y.astype(x.dtype)


def t5_bucket(dist):
    n = jnp.maximum(dist, 0)
    nf = jnp.maximum(n, 1).astype(jnp.float32)
    large = MAX_EXACT + (jnp.log(nf / MAX_EXACT) / math.log(MAX_DISTANCE / MAX_EXACT)
                         * (N_BUCKETS - MAX_EXACT)).astype(jnp.int32)
    large = jnp.minimum(large, N_BUCKETS - 1)
    return jnp.where(n < MAX_EXACT, n, large)


def diff_lambda_init(layer_idx):
    return 0.8 - 0.6 * math.exp(-0.3 * layer_idx)


def sweep_causal_blocks(block_fn, seq_len):
    o_meta = block_fn(0, N_META)
    n_blocks = (seq_len - N_META) // Q_BLOCK
    o_real = lax.map(lambda j: block_fn(N_META + j * Q_BLOCK, Q_BLOCK),
                     jnp.arange(n_blocks, dtype=jnp.int32))
    nb, bsz, _, nh, e = o_real.shape
    o_real = jnp.transpose(o_real, (1, 0, 2, 3, 4)).reshape(bsz, nb * Q_BLOCK, nh, e)
    return jnp.concatenate([o_meta, o_real], axis=1)


def differential_attention(h, w_qkv, lam, subln_w, w_o, rel_bias, lambda_init):
    bsz, seq_len, d = h.shape
    qkv = h @ w_qkv
    q = qkv[..., :d].reshape(bsz, seq_len, N_MAPS_A, HEAD_DIM) * (HEAD_DIM ** -0.5)
    k = qkv[..., d:2 * d].reshape(bsz, seq_len, N_MAPS_A, HEAD_DIM)
    v = qkv[..., 2 * d:].reshape(bsz, seq_len, N_HEADS_A, 2 * HEAD_DIM)
    lamf = lam.astype(jnp.float32)
    lam_full = (jnp.exp(jnp.sum(lamf[0] * lamf[1])) - jnp.exp(jnp.sum(lamf[2] * lamf[3]))
                + lambda_init)
    bias_tab = rel_bias.astype(jnp.float32)
    k_pos = jnp.arange(seq_len, dtype=jnp.int32)

    def block_fn(start, size):
        q_blk = lax.dynamic_slice_in_dim(q, start, size, axis=1)
        q_pos = start + jnp.arange(size, dtype=jnp.int32)
        s = jnp.einsum('bqmd,bkmd->bmqk', q_blk, k, preferred_element_type=jnp.float32)
        dist = q_pos[:, None] - k_pos[None, :]
        s = s + jnp.transpose(bias_tab[t5_bucket(dist)], (2, 0, 1))[None]
        s = jnp.where((dist >= 0)[None, None], s, NEG_INF)
        p = jax.nn.softmax(s, axis=-1).reshape(bsz, N_HEADS_A, 2, size, seq_len)
        a = p[:, :, 0] - lam_full * p[:, :, 1]
        return jnp.einsum('bhqk,bkhe->bqhe', a.astype(v.dtype), v)

    o = sweep_causal_blocks(block_fn, seq_len)
    o = rms_norm(o, subln_w) * (1.0 - lambda_init)
    return o.reshape(bsz, seq_len, d) @ w_o


def forgetting_attention(h, w_in, b_f, w_o):
    bsz, seq_len, d = h.shape
    proj = h @ w_in
    q = proj[..., :d].reshape(bsz, seq_len, N_HEADS_B, HEAD_DIM) * (HEAD_DIM ** -0.5)
    k = proj[..., d:2 * d].reshape(bsz, seq_len, N_HEADS_B, HEAD_DIM)
    v = proj[..., 2 * d:3 * d].reshape(bsz, seq_len, N_HEADS_B, HEAD_DIM)
    log_f = jax.nn.log_sigmoid((proj[..., 3 * d:] + b_f).astype(jnp.float32))
    cum = jnp.transpose(jnp.cumsum(log_f, axis=1), (0, 2, 1))
    k_pos = jnp.arange(seq_len, dtype=jnp.int32)

    def block_fn(start, size):
        q_blk = lax.dynamic_slice_in_dim(q, start, size, axis=1)
        cum_q = lax.dynamic_slice_in_dim(cum, start, size, axis=2)
        q_pos = start + jnp.arange(size, dtype=jnp.int32)
        s = jnp.einsum('bqhd,bkhd->bhqk', q_blk, k, preferred_element_type=jnp.float32)
        s = s + cum_q[:, :, :, None] - cum[:, :, None, :]
        dist = q_pos[:, None] - k_pos[None, :]
        s = jnp.where((dist >= 0)[None, None], s, NEG_INF)
        p = jax.nn.softmax(s, axis=-1)
        return jnp.einsum('bhqk,bkhd->bqhd', p.astype(v.dtype), v)

    o = sweep_causal_blocks(block_fn, seq_len)
    return o.reshape(bsz, seq_len, d) @ w_o


def hierarchical_moe(h, w_rg, b_rg, w_re, b_re, w_gate, w_up, w_down):
    bsz, seq_len, d = h.shape
    t = h.reshape(-1, d)
    g_logits = (t @ w_rg + b_rg).astype(jnp.float32)
    g_prob = jax.nn.softmax(g_logits, axis=-1)
    g_sel = jnp.argmax(g_logits, axis=-1)
    p_group = jnp.take_along_axis(g_prob, g_sel[:, None], axis=1)[:, 0]
    e_logits = (jnp.einsum('nd,dge->nge', t, w_re) + b_re).astype(jnp.float32)
    e_in = jnp.take_along_axis(e_logits, g_sel[:, None, None], axis=1)[:, 0]
    top_v, top_i = lax.top_k(e_in, TOP_K_IN_GROUP)
    w_top = jax.nn.softmax(top_v, axis=-1)
    within = jnp.sum(w_top[..., None] * jax.nn.one_hot(top_i, EXPERTS_PER_GROUP), axis=1)
    gates = (p_group[:, None, None] * jax.nn.one_hot(g_sel, N_GROUPS)[:, :, None]
             * within[:, None, :]).reshape(-1, N_EXPERTS).astype(t.dtype)
    out = jnp.zeros_like(t)
    for e in range(N_EXPERTS):
        u = jax.nn.silu(t @ w_gate[e]) * (t @ w_up[e])
        out = out + gates[:, e:e + 1] * (u @ w_down[e])
    return out.reshape(bsz, seq_len, d)


def setup_inputs(seed: int = 0) -> dict:
    key = jax.random.key(seed)
    ks = jax.random.split(key, 24)
    d = D_MODEL
    nrm = jax.random.normal
    f32 = jnp.float32
    v_scale_a = jnp.concatenate([jnp.ones((2 * d,), f32), jnp.full((d,), DN_BETA, f32)])
    v_scale_b = jnp.concatenate([jnp.ones((2 * d,), f32), jnp.full((d,), DN_BETA, f32),
                                 jnp.ones((N_HEADS_B,), f32)])
    return {
        'x': nrm(ks[0], (BATCH, SEQ, d), f32),
        'meta_tokens': nrm(ks[1], (N_META, d), f32),
        'rel_bias': 0.5 * nrm(ks[2], (N_BUCKETS, N_MAPS_A), f32),
        'diff_w_qkv': nrm(ks[3], (N_LAYERS_A, d, 3 * d), f32) * (d ** -0.5) * v_scale_a,
        'diff_lambda': 0.1 * nrm(ks[4], (N_LAYERS_A, 4, HEAD_DIM), f32),
        'diff_subln': 1.0 + 0.02 * nrm(ks[5], (N_LAYERS_A, 2 * HEAD_DIM), f32),
        'diff_w_o': nrm(ks[6], (N_LAYERS_A, d, d), f32) * (d ** -0.5) * DN_BETA,
        'fox_w_in': nrm(ks[7], (N_LAYERS_B, d, 3 * d + N_HEADS_B), f32) * (d ** -0.5) * v_scale_b,
        'fox_b_f': jax.random.uniform(ks[8], (N_LAYERS_B, N_HEADS_B), f32, minval=1.0, maxval=4.0),
        'fox_w_o': nrm(ks[9], (N_LAYERS_B, d, d), f32) * (d ** -0.5) * DN_BETA,
        'ln_mix_g': 1.0 + 0.02 * nrm(ks[10], (DEPTH, d), f32),
        'ln_mix_b': 0.02 * nrm(ks[11], (DEPTH, d), f32),
        'ln_ffn_g': 1.0 + 0.02 * nrm(ks[12], (DEPTH, d), f32),
        'ln_ffn_b': 0.02 * nrm(ks[13], (DEPTH, d), f32),
        'router_group_w': nrm(ks[14], (DEPTH, d, N_GROUPS), f32) * (d ** -0.5),
        'router_group_b': 0.01 * nrm(ks[15], (DEPTH, N_GROUPS), f32),
        'router_expert_w': nrm(ks[16], (DEPTH, d, N_GROUPS, EXPERTS_PER_GROUP), f32) * (d ** -0.5),
        'router_expert_b': 0.01 * nrm(ks[17], (DEPTH, N_GROUPS, EXPERTS_PER_GROUP), f32),
        'expert_w_gate': nrm(ks[18], (DEPTH, N_EXPERTS, d, D_EXPERT), f32) * (d ** -0.5),
        'expert_w_up': nrm(ks[19], (DEPTH, N_EXPERTS, d, D_EXPERT), f32) * (d ** -0.5),
        'expert_w_down': nrm(ks[20], (DEPTH, N_EXPERTS, D_EXPERT, d), f32) * (D_EXPERT ** -0.5) * DN_BETA,
    }


def reference(x, meta_tokens, rel_bias, diff_w_qkv, diff_lambda, diff_subln, diff_w_o,
              fox_w_in, fox_b_f, fox_w_o, ln_mix_g, ln_mix_b, ln_ffn_g, ln_ffn_b,
              router_group_w, router_group_b, router_expert_w, router_expert_b,
              expert_w_gate, expert_w_up, expert_w_down):
    bsz = x.shape[0]
    meta = jnp.broadcast_to(meta_tokens[None].astype(x.dtype), (bsz, N_META, x.shape[-1]))
    h = jnp.concatenate([meta, x], axis=1)
    for i in range(DEPTH):
        j = i // N_MIXERS
        if i % N_MIXERS == 0:
            mix = differential_attention(h, diff_w_qkv[j], diff_lambda[j], diff_subln[j],
                                         diff_w_o[j], rel_bias, diff_lambda_init(i))
        else:
            mix = forgetting_attention(h, fox_w_in[j], fox_b_f[j], fox_w_o[j])
        h = layer_norm(DN_ALPHA * h + mix, ln_mix_g[i], ln_mix_b[i])
        ffn = hierarchical_moe(h, router_group_w[i], router_group_b[i], router_expert_w[i],
                               router_expert_b[i], expert_w_gate[i], expert_w_up[i], expert_w_down[i])
        h = layer_norm(DN_ALPHA * h + ffn, ln_ffn_g[i], ln_ffn_b[i])
    return h[:, N_META:]
```

```python
import functools
import math

import numpy as np
import jax
import jax.numpy as jnp
from jax import lax
from jax.experimental import pallas as pl
from jax.experimental.pallas import tpu as pltpu

F32 = jnp.float32
BF16 = jnp.bfloat16

N_META = 16
HEAD_DIM = 64
N_PAIRS = 8
LANES = 128
N_BUCKETS = 32
MAX_EXACT = 16
MAX_DISTANCE = 128
N_GROUPS = 4
EXPERTS_PER_GROUP = 8
N_EXPERTS = 32
D_EXPERT = 512
DEPTH = 2
DN_ALPHA = (2 * DEPTH) ** 0.25
LN_EPS = 1e-5
NEG_INF = -1e30

ROW_TILE = 512
ATT_TILE = 512
EXP_TILE = 512
VMEM_LIMIT = 48 * 1024 * 1024


def _lambda_init(layer_idx):
    return 0.8 - 0.6 * math.exp(-0.3 * layer_idx)


def _bucket_table(n):
    d = np.arange(n)
    nf = np.maximum(d, 1).astype(np.float64)
    large = MAX_EXACT + (np.log(nf / MAX_EXACT) / math.log(MAX_DISTANCE / MAX_EXACT)
                         * (N_BUCKETS - MAX_EXACT)).astype(np.int64)
    large = np.minimum(large, N_BUCKETS - 1)
    return np.where(d < MAX_EXACT, d, large).astype(np.int32)


def _proj_body(x_ref, w_ref, q_ref, k_ref, v_ref):
    d = x_ref.shape[1]
    xb = x_ref[...].astype(BF16)
    for i, o_ref in enumerate((q_ref, k_ref, v_ref)):
        acc = jnp.dot(xb, w_ref[:, i * d:(i + 1) * d], preferred_element_type=F32)
        if i == 0:
            acc = acc * (HEAD_DIM ** -0.5)
        for j in range(N_PAIRS):
            o_ref[j] = acc[:, j * LANES:(j + 1) * LANES].astype(BF16)
    return xb


def _proj_kernel(x_ref, w_ref, q_ref, k_ref, v_ref):
    _proj_body(x_ref, w_ref, q_ref, k_ref, v_ref)


def _proj_fox_kernel(x_ref, w_ref, wf_ref, bf_ref, q_ref, k_ref, v_ref, lft_ref):
    xb = _proj_body(x_ref, w_ref, q_ref, k_ref, v_ref)
    f = jnp.dot(xb, wf_ref[...], preferred_element_type=F32)
    z = f.T[:lft_ref.shape[0], :] + bf_ref[...]
    lft_ref[...] = jnp.minimum(z, 0.0) - jnp.log1p(jnp.exp(-jnp.abs(z)))


def _project(h, w_bf, wf_bf=None, bf_col=None):
    r, d = h.shape
    tm = ROW_TILE
    qkv_shape = jax.ShapeDtypeStruct((N_PAIRS, r, LANES), BF16)
    qkv_spec = pl.BlockSpec((N_PAIRS, tm, LANES), lambda i: (0, i, 0))
    in_specs = [pl.BlockSpec((tm, d), lambda i: (i, 0)),
                pl.BlockSpec((d, 3 * d), lambda i: (0, 0))]
    params = pltpu.CompilerParams(dimension_semantics=("arbitrary",),
                                  vmem_limit_bytes=VMEM_LIMIT)
    if wf_bf is None:
        return pl.pallas_call(
            _proj_kernel, grid=(r // tm,), in_specs=in_specs,
            out_specs=[qkv_spec] * 3, out_shape=[qkv_shape] * 3,
            compiler_params=params, name="proj_diff")(h, w_bf)
    nh = bf_col.shape[0]
    in_specs += [pl.BlockSpec((d, LANES), lambda i: (0, 0)),
                 pl.BlockSpec((nh, 1), lambda i: (0, 0))]
    return pl.pallas_call(
        _proj_fox_kernel, grid=(r // tm,), in_specs=in_specs,
        out_specs=[qkv_spec] * 3 + [pl.BlockSpec((nh, tm), lambda i: (0, i))],
        out_shape=[qkv_shape] * 3 + [jax.ShapeDtypeStruct((nh, r), F32)],
        compiler_params=params, name="proj_fox")(h, w_bf, wf_bf, bf_col)


def _cumsum_kernel(lfm_ref, lft_ref, cmeta_ref, ct_ref, carry_ref):
    c = pl.program_id(1)
    tc = lft_ref.shape[1]

    @pl.when(c == 0)
    def _():
        lm = lfm_ref[0]
        n = lm.shape[1]
        tri = (lax.broadcasted_iota(jnp.int32, (n, n), 0)
               <= lax.broadcasted_iota(jnp.int32, (n, n), 1)).astype(F32)
        cm = jnp.dot(lm, tri, preferred_element_type=F32, precision=lax.Precision.HIGHEST)
        cmeta_ref[0] = cm
        carry_ref[...] = cm[:, n - 1:n]

    tri = (lax.broadcasted_iota(jnp.int32, (tc, tc), 0)
           <= lax.broadcasted_iota(jnp.int32, (tc, tc), 1)).astype(F32)
    cs = jnp.dot(lft_ref[...], tri, preferred_element_type=F32,
                 precision=lax.Precision.HIGHEST) + carry_ref[...]
    ct_ref[...] = cs
    carry_ref[...] = cs[:, tc - 1:tc]


def _fox_cumsum(lft, bsz, seq):
    nh, _ = lft.shape
    rr = bsz * seq
    tc = ROW_TILE
    nc = seq // tc
    lf_meta = lft[:, rr:rr + bsz * N_META].reshape(nh, bsz, N_META).transpose(1, 0, 2)
    return pl.pallas_call(
        _cumsum_kernel, grid=(bsz, nc),
        in_specs=[pl.BlockSpec((1, nh, N_META), lambda b, c: (b, 0, 0)),
                  pl.BlockSpec((nh, tc), lambda b, c: (0, b * nc + c))],
        out_specs=[pl.BlockSpec((1, nh, N_META), lambda b, c: (b, 0, 0)),
                   pl.BlockSpec((nh, tc), lambda b, c: (0, b * nc + c))],
        out_shape=[jax.ShapeDtypeStruct((bsz, nh, N_META), F32),
                   jax.ShapeDtypeStruct((nh, rr), F32)],
        scratch_shapes=[pltpu.VMEM((nh, 1), F32)],
        compiler_params=pltpu.CompilerParams(dimension_semantics=("arbitrary", "arbitrary")),
        name="fox_cumsum")(lf_meta, lft)


def _masked_q(qj):
    lane = lax.broadcasted_iota(jnp.int32, qj.shape, 1)
    zero = jnp.zeros_like(qj)
    return jnp.where(lane < HEAD_DIM, qj, zero), jnp.where(lane >= HEAD_DIM, qj, zero)


def _scores(qm, kj):
    return lax.dot_general(qm, kj, (((1,), (1,)), ((), ())), preferred_element_type=F32)


def _diag_bias(d0, d1, n):
    zero = jnp.zeros_like(d0)
    neg = jnp.full_like(d0, NEG_INF)
    rows = []
    for r in range(n):
        blocks = []
        for c in range(n):
            if c > r:
                blocks.append(neg)
            elif c == r:
                blocks.append(d0)
            elif c == r - 1 and d1 is not None:
                blocks.append(d1)
            else:
                blocks.append(zero)
        rows.append(jnp.concatenate(blocks, axis=1))
    return jnp.concatenate(rows, axis=0)


def _causal_block():
    row = lax.broadcasted_iota(jnp.int32, (LANES, LANES), 0)
    col = lax.broadcasted_iota(jnp.int32, (LANES, LANES), 1)
    return jnp.where(col > row, NEG_INF, 0.0).astype(F32)


def _diff_lambda(lam_ref, lambda_init):
    lam = lam_ref[...]
    a = jnp.sum(lam[0:1] * lam[1:2], axis=1, keepdims=True)
    b = jnp.sum(lam[2:3] * lam[3:4], axis=1, keepdims=True)
    return jnp.exp(a) - jnp.exp(b) + lambda_init


def _diff_finish(acc0, l0, acc1, l1, lam_full, subln, lambda_init):
    o = acc0 / l0 - lam_full * (acc1 / l1)
    ms = jnp.mean(o * o, axis=1, keepdims=True)
    return o * lax.rsqrt(ms + LN_EPS) * subln * (1.0 - lambda_init)


def _fox_finish(acc0, l0, acc1, l1):
    lane = lax.broadcasted_iota(jnp.int32, acc0.shape, 1)
    return jnp.where(lane < HEAD_DIM, acc0 / l0, acc1 / l1)


def _attn_kernel(qmap_ref, kmap_ref, q_ref, k_ref, v_ref, km_ref, vm_ref, *rest,
                 fox, lambda_init):
    if fox:
        ck_ref, ckm_ref, o_ref, m_sc, l_sc, acc_sc = rest
    else:
        dtab_ref, mtab_ref, lam_ref, subln_ref, o_ref, m_sc, l_sc, acc_sc = rest
    t = pl.program_id(1)
    qi = qmap_ref[t]
    ki = kmap_ref[t]
    tq = q_ref.shape[1]
    nblk = tq // LANES

    def init_from_meta(j, c, qm, near):
        mi = 2 * j + c
        s = _scores(qm, km_ref[j])
        if fox:
            s = s - ckm_ref[0, pl.ds(mi, 1), :]
        elif near:
            mt = mtab_ref[mi]
            s = s + jnp.concatenate(
                [mt, jnp.zeros((tq - mt.shape[0], mt.shape[1]), F32)], axis=0)
        m = jnp.max(s, axis=1, keepdims=True)
        p = jnp.exp(s - m)
        m_sc[mi] = m
        l_sc[mi] = jnp.sum(p, axis=1, keepdims=True)
        acc_sc[mi] = jnp.dot(p.astype(BF16), vm_ref[j], preferred_element_type=F32)

    def update(j, c, qm, kind):
        mi = 2 * j + c
        s = _scores(qm, k_ref[j])
        if fox:
            s = s - ck_ref[pl.ds(mi, 1), :]
            if kind == "diag":
                s = s + _diag_bias(_causal_block(), None, nblk)
        else:
            if kind == "diag":
                s = s + _diag_bias(dtab_ref[mi, 0], dtab_ref[mi, 1], nblk)
            elif kind == "sub":
                top = jnp.concatenate(
                    [jnp.zeros((LANES, tq - LANES), F32), dtab_ref[mi, 1]], axis=1)
                s = s + jnp.concatenate(
                    [top, jnp.zeros((tq - LANES, tq), F32)], axis=0)
        m_prev = m_sc[mi]
        m_new = jnp.maximum(m_prev, jnp.max(s, axis=1, keepdims=True))
        alpha = jnp.exp(m_prev - m_new)
        p = jnp.exp(s - m_new)
        l_sc[mi] = alpha * l_sc[mi] + jnp.sum(p, axis=1, keepdims=True)
        acc_sc[mi] = alpha * acc_sc[mi] + jnp.dot(p.astype(BF16), v_ref[j],
                                                  preferred_element_type=F32)
        m_sc[mi] = m_new

    def run(kind, first, near_meta=False):
        def pair(j, carry):
            qm = _masked_q(q_ref[j])
            for c in range(2):
                if first:
                    init_from_meta(j, c, qm[c], near_meta)
                update(j, c, qm[c], kind)
            if kind == "diag":
                a0, a1 = acc_sc[2 * j], acc_sc[2 * j + 1]
                l0, l1 = l_sc[2 * j], l_sc[2 * j + 1]
                if fox:
                    o = _fox_finish(a0, l0, a1, l1)
                else:
                    o = _diff_finish(a0, l0, a1, l1, _diff_lambda(lam_ref, lambda_init),
                                     subln_ref[...], lambda_init)
                o_ref[j] = o.astype(o_ref.dtype)
            return carry
        lax.fori_loop(0, N_PAIRS, pair, 0)

    @pl.when(jnp.logical_and(qi == 0, ki == 0))
    def _():
        run("diag", first=True, near_meta=True)

    @pl.when(jnp.logical_and(qi > 0, ki == 0))
    def _():
        if fox:
            run("far", first=True)
        else:
            @pl.when(qi == 1)
            def _():
                run("sub", first=True)

            @pl.when(qi > 1)
            def _():
                run("far", first=True)

    @pl.when(jnp.logical_and(ki > 0, ki == qi))
    def _():
        run("diag", first=False)

    @pl.when(jnp.logical_and(ki > 0, ki < qi))
    def _():
        if fox:
            run("far", first=False)
        else:
            @pl.when(ki == qi - 1)
            def _():
                run("sub", first=False)

            @pl.when(ki < qi - 1)
            def _():
                run("far", first=False)


def _attn_meta_kernel(q_ref, k_ref, v_ref, *rest, fox, lambda_init):
    if fox:
        ckm_ref, o_ref = rest
    else:
        dtab_ref, lam_ref, subln_ref, o_ref = rest
    b = pl.program_id(0)

    @pl.when(b == 0)
    def _():
        o_ref[...] = jnp.zeros_like(o_ref)

    rows = pl.ds(pl.multiple_of(b * N_META, N_META), N_META)
    causal = _causal_block()[:N_META, :N_META]

    def pair(j, carry):
        qm = _masked_q(q_ref[j, rows, :])
        kj = k_ref[j, rows, :]
        vj = v_ref[j, rows, :]
        acc, l = [], []
        for c in range(2):
            mi = 2 * j + c
            s = _scores(qm[c], kj)
            if fox:
                s = s - ckm_ref[b, pl.ds(mi, 1), :] + causal
            else:
                s = s + dtab_ref[mi, 0][:N_META, :N_META]
            m = jnp.max(s, axis=1, keepdims=True)
            p = jnp.exp(s - m)
            l.append(jnp.sum(p, axis=1, keepdims=True))
            acc.append(jnp.dot(p.astype(BF16), vj, preferred_element_type=F32))
        if fox:
            o = _fox_finish(acc[0], l[0], acc[1], l[1])
        else:
            o = _diff_finish(acc[0], l[0], acc[1], l[1], _diff_lambda(lam_ref, lambda_init),
                             subln_ref[...], lambda_init)
        o_ref[j, rows, :] = o.astype(o_ref.dtype)
        return carry

    lax.fori_loop(0, N_PAIRS, pair, 0)


def _attention(q, k, v, bsz, seq, *, fox, lambda_init=0.0, extras=()):
    _, r, _ = q.shape
    rr = bsz * seq
    tq = ATT_TILE
    nq = seq // tq
    qmap = np.concatenate([np.full(i + 1, i) for i in range(nq)]).astype(np.int32)
    kmap = np.concatenate([np.arange(i + 1) for i in range(nq)]).astype(np.int32)
    nsteps = int(qmap.shape[0])
    meta_blk = rr // N_META

    tile_q = pl.BlockSpec((N_PAIRS, tq, LANES), lambda b, t, qm, km: (0, b * nq + qm[t], 0))
    tile_k = pl.BlockSpec((N_PAIRS, tq, LANES), lambda b, t, qm, km: (0, b * nq + km[t], 0))
    tile_m = pl.BlockSpec((N_PAIRS, N_META, LANES), lambda b, t, qm, km: (0, meta_blk + b, 0))

    def const_spec(a):
        nd = a.ndim
        return pl.BlockSpec(a.shape, lambda b, t, qm, km: (0,) * nd)

    if fox:
        ct, cmeta = extras
        nh = ct.shape[0]
        extra_specs = [pl.BlockSpec((nh, tq), lambda b, t, qm, km: (0, b * nq + km[t])),
                       pl.BlockSpec((1, nh, N_META), lambda b, t, qm, km: (b, 0, 0))]
    else:
        extra_specs = [const_spec(a) for a in extras]

    o_main = pl.pallas_call(
        functools.partial(_attn_kernel, fox=fox, lambda_init=lambda_init),
        grid_spec=pltpu.PrefetchScalarGridSpec(
            num_scalar_prefetch=2, grid=(bsz, nsteps),
            in_specs=[tile_q, tile_k, tile_k, tile_m, tile_m] + extra_specs,
            out_specs=tile_q,
            scratch_shapes=[pltpu.VMEM((2 * N_PAIRS, tq, 1), F32),
                            pltpu.VMEM((2 * N_PAIRS, tq, 1), F32),
                            pltpu.VMEM((2 * N_PAIRS, tq, LANES), F32)]),
        out_shape=jax.ShapeDtypeStruct((N_PAIRS, rr, LANES), BF16),
        compiler_params=pltpu.CompilerParams(
            dimension_semantics=("arbitrary", "arbitrary"), vmem_limit_bytes=VMEM_LIMIT),
        name="attn_fox" if fox else "attn_diff",
    )(jnp.asarray(qmap), jnp.asarray(kmap), q, k, v, k, v, *extras)

    tail = pl.BlockSpec((N_PAIRS, r - rr, LANES), lambda b: (0, rr // (r - rr), 0))
    if fox:
        m_extras = (extras[1],)
    else:
        m_extras = (extras[0], extras[2], extras[3])
    m_specs = [pl.BlockSpec(a.shape, functools.partial(lambda nd, b: (0,) * nd, a.ndim))
               for a in m_extras]
    o_meta = pl.pallas_call(
        functools.partial(_attn_meta_kernel, fox=fox, lambda_init=lambda_init),
        grid=(bsz,),
        in_specs=[tail, tail, tail] + m_specs,
        out_specs=pl.BlockSpec((N_PAIRS, r - rr, LANES), lambda b: (0, 0, 0)),
        out_shape=jax.ShapeDtypeStruct((N_PAIRS, r - rr, LANES), BF16),
        compiler_params=pltpu.CompilerParams(dimension_semantics=("arbitrary",)),
        name="attn_fox_meta" if fox else "attn_diff_meta",
    )(q, k, v, *m_extras)
    return o_main, o_meta


def _layer_norm(y, g, b):
    mu = jnp.mean(y, axis=1, keepdims=True)
    yc = y - mu
    var = jnp.mean(yc * yc, axis=1, keepdims=True)
    return yc * lax.rsqrt(var + LN_EPS) * g + b


def _mix_router_kernel(o_ref, om_ref, h_ref, wo_ref, g_ref, b_ref, wr_ref, br_ref,
                       h1_ref, route_ref, counts_ref, carry_ref):
    i = pl.program_id(0)
    tm = h_ref.shape[0]

    @pl.when(i == 0)
    def _():
        carry_ref[...] = jnp.zeros_like(carry_ref)

    is_tail = i == pl.num_programs(0) - 1
    o = jnp.concatenate([jnp.where(is_tail, om_ref[j], o_ref[j]) for j in range(N_PAIRS)],
                        axis=1)
    mix = jnp.dot(o, wo_ref[...], preferred_element_type=F32)
    h1 = _layer_norm(DN_ALPHA * h_ref[...] + mix, g_ref[...], b_ref[...])
    h1_ref[...] = h1

    logits = jnp.dot(h1.astype(BF16), wr_ref[...], preferred_element_type=F32) + br_ref[...]
    lane = lax.broadcasted_iota(jnp.int32, logits.shape, 1)
    lane_f = lane.astype(F32)
    big = float(LANES)
    g_mask = lane < N_GROUPS
    lg = jnp.where(g_mask, logits, NEG_INF)
    gmax = jnp.max(lg, axis=1, keepdims=True)
    gsel = jnp.min(jnp.where(lg == gmax, lane_f, big), axis=1, keepdims=True)
    p_group = 1.0 / jnp.sum(jnp.where(g_mask, jnp.exp(lg - gmax), 0.0), axis=1, keepdims=True)
    e_lo = N_GROUPS + EXPERTS_PER_GROUP * gsel
    e_mask = jnp.logical_and(lane_f >= e_lo, lane_f < e_lo + EXPERTS_PER_GROUP)
    le = jnp.where(e_mask, logits, NEG_INF)
    v1 = jnp.max(le, axis=1, keepdims=True)
    i1 = jnp.min(jnp.where(jnp.logical_and(le == v1, e_mask), lane_f, big),
                 axis=1, keepdims=True)
    e_mask2 = jnp.logical_and(e_mask, lane_f != i1)
    le2 = jnp.where(e_mask2, logits, NEG_INF)
    v2 = jnp.max(le2, axis=1, keepdims=True)
    i2 = jnp.min(jnp.where(jnp.logical_and(le2 == v2, e_mask2), lane_f, big),
                 axis=1, keepdims=True)
    ex = jnp.exp(v2 - v1)
    gate0 = p_group * (1.0 / (1.0 + ex))
    gate1 = p_group * (ex / (1.0 + ex))
    e0 = i1 - N_GROUPS
    e1 = i2 - N_GROUPS

    onehot = jnp.logical_or(lane_f == e0, lane_f == e1)
    oh_bf = jnp.where(onehot, 1.0, 0.0).astype(BF16)
    ltri = jnp.where(lax.broadcasted_iota(jnp.int32, (tm, tm), 1)
                     < lax.broadcasted_iota(jnp.int32, (tm, tm), 0), 1.0, 0.0).astype(BF16)
    before = jnp.dot(ltri, oh_bf, preferred_element_type=F32) + carry_ref[...]
    r0 = jnp.sum(jnp.where(lane_f == e0, before, 0.0), axis=1, keepdims=True)
    r1 = jnp.sum(jnp.where(lane_f == e1, before, 0.0), axis=1, keepdims=True)
    carry = carry_ref[...] + jnp.sum(jnp.where(onehot, 1.0, 0.0), axis=0, keepdims=True)
    carry_ref[...] = carry
    counts_ref[...] = carry

    route = jnp.zeros_like(logits)
    for col, val in enumerate((e0, e1, r0, r1, gate0, gate1)):
        route = jnp.where(lane == col, val, route)
    route_ref[...] = route


def _mix_and_route(o, o_meta, h, wo_bf, ln_g, ln_b, wr_bf, br):
    r, d = h.shape
    tm = ROW_TILE
    assert o_meta.shape[1] == tm and o.shape[1] == r - tm
    last_real = o.shape[1] // tm - 1
    row = lambda i: (i, 0)
    const = lambda i: (0, 0)
    return pl.pallas_call(
        _mix_router_kernel, grid=(r // tm,),
        in_specs=[pl.BlockSpec((N_PAIRS, tm, LANES), lambda i: (0, jnp.minimum(i, last_real), 0)),
                  pl.BlockSpec((N_PAIRS, tm, LANES), lambda i: (0, 0, 0)),
                  pl.BlockSpec((tm, d), row),
                  pl.BlockSpec((d, d), const),
                  pl.BlockSpec((1, d), const), pl.BlockSpec((1, d), const),
                  pl.BlockSpec((d, LANES), const), pl.BlockSpec((1, LANES), const)],
        out_specs=[pl.BlockSpec((tm, d), row), pl.BlockSpec((tm, LANES), row),
                   pl.BlockSpec((1, LANES), const)],
        out_shape=[jax.ShapeDtypeStruct((r, d), F32), jax.ShapeDtypeStruct((r, LANES), F32),
                   jax.ShapeDtypeStruct((1, LANES), F32)],
        scratch_shapes=[pltpu.VMEM((1, LANES), F32)],
        compiler_params=pltpu.CompilerParams(dimension_semantics=("arbitrary",),
                                             vmem_limit_bytes=VMEM_LIMIT),
        name="mix_router")(o, o_meta, h, wo_bf, ln_g, ln_b, wr_bf, br)


def _row_copy(src_ref, src_row, dst_ref, dst_row, sem):
    return pltpu.make_async_copy(src_ref.at[pl.ds(src_row, 1), :],
                                 dst_ref.at[pl.ds(dst_row, 1), :], sem)


def _dispatch_kernel(slots_ref, h_ref, xs_in_ref, xs_ref, slot_smem, sem, idx_sem):
    del xs_in_ref
    i = pl.program_id(0)
    tm = slot_smem.shape[0] // 2
    idx_copy = pltpu.make_async_copy(slots_ref.at[i], slot_smem, idx_sem)
    idx_copy.start()
    idx_copy.wait()

    def start(t, carry):
        for c in range(2):
            _row_copy(h_ref, i * tm + t, xs_ref, slot_smem[c * tm + t], sem).start()
        return carry

    def wait(t, carry):
        for c in range(2):
            _row_copy(h_ref, i * tm + t, xs_ref, slot_smem[c * tm + t], sem).wait()
        return carry

    lax.fori_loop(0, tm, start, 0)
    lax.fori_loop(0, tm, wait, 0)


def _dispatch(h1, slots, n_rows):
    r, d = h1.shape
    tm = ROW_TILE
    xs0 = jnp.zeros((n_rows, d), F32)
    any_spec = pl.BlockSpec(memory_space=pl.ANY)
    return pl.pallas_call(
        _dispatch_kernel, grid=(r // tm,),
        in_specs=[any_spec, any_spec, any_spec], out_specs=any_spec,
        out_shape=jax.ShapeDtypeStruct((n_rows, d), F32),
        scratch_shapes=[pltpu.SMEM((2 * tm,), jnp.int32), pltpu.SemaphoreType.DMA,
                        pltpu.SemaphoreType.DMA],
        input_output_aliases={2: 0},
        compiler_params=pltpu.CompilerParams(dimension_semantics=("arbitrary",)),
        name="moe_dispatch")(slots, h1, xs0)


def _expert_kernel(tile_e_ref, n_used_ref, x_ref, wg_ref, wu_ref, wd_ref, y_ref):
    del tile_e_ref

    @pl.when(pl.program_id(0) < n_used_ref[0])
    def _():
        xb = x_ref[...].astype(BF16)
        g = jnp.dot(xb, wg_ref[0], preferred_element_type=F32)
        u = jnp.dot(xb, wu_ref[0], preferred_element_type=F32)
        a = (g * jax.nn.sigmoid(g) * u).astype(BF16)
        y_ref[...] = jnp.dot(a, wd_ref[0], preferred_element_type=F32)

    @pl.when(pl.program_id(0) >= n_used_ref[0])
    def _():
        y_ref[...] = jnp.zeros_like(y_ref)


def _experts(xs, tile_e, n_used, wg_bf, wu_bf, wd_bf):
    p, d = xs.shape
    te = EXP_TILE
    de = wg_bf.shape[2]
    row = lambda i, te_ref, nu_ref: (jnp.minimum(i, nu_ref[0] - 1), 0)
    wsel = lambda i, te_ref, nu_ref: (te_ref[i], 0, 0)
    return pl.pallas_call(
        _expert_kernel,
        grid_spec=pltpu.PrefetchScalarGridSpec(
            num_scalar_prefetch=2, grid=(p // te,),
            in_specs=[pl.BlockSpec((te, d), row),
                      pl.BlockSpec((1, d, de), wsel), pl.BlockSpec((1, d, de), wsel),
                      pl.BlockSpec((1, de, d), wsel)],
            out_specs=pl.BlockSpec((te, d), lambda i, te_ref, nu_ref: (i, 0))),
        out_shape=jax.ShapeDtypeStruct((p, d), F32),
        compiler_params=pltpu.CompilerParams(dimension_semantics=("arbitrary",),
                                             vmem_limit_bytes=VMEM_LIMIT),
        name="moe_experts")(tile_e, n_used, xs, wg_bf, wu_bf, wd_bf)


def _combine_kernel(slots_ref, ys_ref, route_ref, h_ref, g_ref, b_ref, out_ref,
                    slot_smem, y0_ref, y1_ref, sem, idx_sem):
    i = pl.program_id(0)
    tm = h_ref.shape[0]
    idx_copy = pltpu.make_async_copy(slots_ref.at[i], slot_smem, idx_sem)
    idx_copy.start()
    idx_copy.wait()
    bufs = (y0_ref, y1_ref)

    def start(t, carry):
        for c in range(2):
            _row_copy(ys_ref, slot_smem[c * tm + t], bufs[c], t, sem).start()
        return carry

    def wait(t, carry):
        for c in range(2):
            _row_copy(ys_ref, slot_smem[c * tm + t], bufs[c], t, sem).wait()
        return carry

    lax.fori_loop(0, tm, start, 0)
    lax.fori_loop(0, tm, wait, 0)
    route = route_ref[...]
    ffn = route[:, 4:5] * y0_ref[...] + route[:, 5:6] * y1_ref[...]
    out_ref[...] = _layer_norm(DN_ALPHA * h_ref[...] + ffn, g_ref[...], b_ref[...])


def _combine(ys, slots, route, h1, ln_g, ln_b, n_out_rows):
    r, d = h1.shape
    tm = ROW_TILE
    row = lambda i: (i, 0)
    const = lambda i: (0, 0)
    any_spec = pl.BlockSpec(memory_space=pl.ANY)
    return pl.pallas_call(
        _combine_kernel, grid=(n_out_rows // tm,),
        in_specs=[any_spec, any_spec, pl.BlockSpec((tm, LANES), row),
                  pl.BlockSpec((tm, d), row), pl.BlockSpec((1, d), const),
                  pl.BlockSpec((1, d), const)],
        out_specs=pl.BlockSpec((tm, d), row),
        out_shape=jax.ShapeDtypeStruct((n_out_rows, d), F32),
        scratch_shapes=[pltpu.SMEM((2 * tm,), jnp.int32), pltpu.VMEM((tm, d), F32),
                        pltpu.VMEM((tm, d), F32), pltpu.SemaphoreType.DMA,
                        pltpu.SemaphoreType.DMA],
        compiler_params=pltpu.CompilerParams(dimension_semantics=("arbitrary",),
                                             vmem_limit_bytes=VMEM_LIMIT),
        name="moe_combine")(slots, ys, route, h1, ln_g, ln_b)


def _moe(h1, route, counts, wg_bf, wu_bf, wd_bf, ln_g, ln_b, n_out_rows):
    r, _ = h1.shape
    tm, te = ROW_TILE, EXP_TILE
    n_tiles = (2 * r) // te + N_EXPERTS
    cnt = counts[0, :N_EXPERTS].astype(jnp.int32)
    seg_tiles = (cnt + te - 1) // te
    seg_end = jnp.cumsum(seg_tiles)
    seg_start = (seg_end - seg_tiles) * te
    n_used = seg_end[-1:]
    tile_e = jnp.searchsorted(seg_end, jnp.arange(n_tiles, dtype=jnp.int32), side="right")
    last_e = jnp.searchsorted(seg_end, n_used[0] - 1, side="right")
    tile_e = jnp.minimum(tile_e, last_e).astype(jnp.int32)
    e01 = route[:, 0:2].astype(jnp.int32)
    slot = seg_start[e01] + route[:, 2:4].astype(jnp.int32)
    slots = slot.reshape(r // tm, tm, 2).transpose(0, 2, 1).reshape(r // tm, 2 * tm)

    xs = _dispatch(h1, slots, n_tiles * te)
    ys = _experts(xs, tile_e, n_used.astype(jnp.int32), wg_bf, wu_bf, wd_bf)
    return _combine(ys, slots, route, h1, ln_g, ln_b, n_out_rows)


def _diff_tables(rel_bias):
    bkt = _bucket_table(2 * LANES + N_META)
    rel = (rel_bias - rel_bias[N_BUCKETS - 1:N_BUCKETS]).T
    i = np.arange(LANES)[:, None]
    j = np.arange(LANES)[None, :]
    d0 = jnp.where(jnp.asarray(i >= j), rel[:, bkt[np.maximum(i - j, 0)]], NEG_INF)
    d1 = rel[:, bkt[LANES + i - j]]
    dtab = jnp.stack([d0, d1], axis=1)
    jm = np.arange(N_META)[None, :]
    mtab = rel[:, bkt[N_META + i - jm]]
    return dtab.astype(F32), mtab.astype(F32)


def kernel(x, meta_tokens, rel_bias, diff_w_qkv, diff_lambda, diff_subln, diff_w_o,
           fox_w_in, fox_b_f, fox_w_o, ln_mix_g, ln_mix_b, ln_ffn_g, ln_ffn_b,
           router_group_w, router_group_b, router_expert_w, router_expert_b,
           expert_w_gate, expert_w_up, expert_w_down):
    bsz, seq, d = x.shape
    rr = bsz * seq
    r = rr + ROW_TILE
    assert seq % ATT_TILE == 0 and bsz * N_META <= ROW_TILE and d == N_PAIRS * LANES
    n_pad = r - rr - bsz * N_META
    h = jnp.concatenate([x.reshape(rr, d),
                         jnp.tile(meta_tokens.astype(x.dtype), (bsz, 1)),
                         jnp.zeros((n_pad, d), x.dtype)], axis=0)

    for i in range(DEPTH):
        j = i // 2
        if i % 2 == 0:
            lam0 = _lambda_init(i)
            q, k, v = _project(h, diff_w_qkv[j].astype(BF16))
            dtab, mtab = _diff_tables(rel_bias)
            o, o_meta = _attention(q, k, v, bsz, seq, fox=False, lambda_init=lam0,
                           extras=(dtab, mtab, diff_lambda[j].astype(F32),
                                   diff_subln[j].astype(F32)[None, :]))
            w_o = diff_w_o[j]
        else:
            w_in = fox_w_in[j]
            nh = w_in.shape[1] - 3 * d
            wf = jnp.pad(w_in[:, 3 * d:], ((0, 0), (0, LANES - nh))).astype(BF16)
            q, k, v, lft = _project(h, w_in[:, :3 * d].astype(BF16), wf,
                                    fox_b_f[j].astype(F32)[:, None])
            cmeta, ct = _fox_cumsum(lft, bsz, seq)
            o, o_meta = _attention(q, k, v, bsz, seq, fox=True, extras=(ct, cmeta))
            w_o = fox_w_o[j]

        n_router = N_GROUPS + N_EXPERTS
        wr = jnp.concatenate([router_group_w[i], router_expert_w[i].reshape(d, N_EXPERTS)], axis=1)
        wr = jnp.pad(wr, ((0, 0), (0, LANES - n_router))).astype(BF16)
        br = jnp.concatenate([router_group_b[i], router_expert_b[i].reshape(N_EXPERTS)])
        br = jnp.pad(br, (0, LANES - n_router)).astype(F32)[None, :]
        h1, route, counts = _mix_and_route(o, o_meta, h, w_o.astype(BF16), ln_mix_g[i][None, :],
                                           ln_mix_b[i][None, :], wr, br)
        n_out = rr if i == DEPTH - 1 else r
        h = _moe(h1, route, counts, expert_w_gate[i].astype(BF16), expert_w_up[i].astype(BF16),
                 expert_w_down[i].astype(BF16), ln_ffn_g[i][None, :], ln_ffn_b[i][None, :], n_out)

    return h.reshape(bsz, seq, d)
```

```python
import functools
import math

import numpy as np
import jax
import jax.numpy as jnp
from jax import lax
from jax.experimental import pallas as pl
from jax.experimental.pallas import tpu as pltpu

F32 = jnp.float32
BF16 = jnp.bfloat16

N_META = 16
HEAD_DIM = 64
N_PAIRS = 8
LANES = 128
N_BUCKETS = 32
MAX_EXACT = 16
MAX_DISTANCE = 128
N_GROUPS = 4
EXPERTS_PER_GROUP = 8
N_EXPERTS = 32
DEPTH = 2
DN_ALPHA = (2 * DEPTH) ** 0.25
LN_EPS = 1e-5
NEG_INF = -1e30
CK_PIECES = 3

ROW_TILE = 512
ATT_TILE = 512
PAIR_UNROLL = 2
EXP_TILE = 512
VMEM_LIMIT = 48 * 1024 * 1024

_NT = (((1,), (1,)), ((), ()))
_TN = (((0,), (0,)), ((), ()))


def _lambda_init(layer_idx):
    return 0.8 - 0.6 * math.exp(-0.3 * layer_idx)


def _bucket_table(n):
    d = np.arange(n)
    nf = np.maximum(d, 1).astype(np.float64)
    large = MAX_EXACT + (np.log(nf / MAX_EXACT) / math.log(MAX_DISTANCE / MAX_EXACT)
                         * (N_BUCKETS - MAX_EXACT)).astype(np.int64)
    large = np.minimum(large, N_BUCKETS - 1)
    return np.where(d < MAX_EXACT, d, large).astype(np.int32)


def _proj_body(x_ref, wqt_ref, wk_ref, wvt_ref, qt_ref, k_ref, vt_ref):
    xb = x_ref[...].astype(BF16)
    qt = lax.dot_general(wqt_ref[...], xb, _NT, preferred_element_type=F32) * (HEAD_DIM ** -0.5)
    kk = jnp.dot(xb, wk_ref[...], preferred_element_type=F32)
    vt = lax.dot_general(wvt_ref[...], xb, _NT, preferred_element_type=F32)
    for j in range(N_PAIRS):
        sl = slice(j * LANES, (j + 1) * LANES)
        qt_ref[j] = qt[sl, :].astype(BF16)
        k_ref[j] = kk[:, sl].astype(BF16)
        vt_ref[j] = vt[sl, :].astype(BF16)
    return xb


def _proj_kernel(x_ref, wqt_ref, wk_ref, wvt_ref, qt_ref, k_ref, vt_ref):
    _proj_body(x_ref, wqt_ref, wk_ref, wvt_ref, qt_ref, k_ref, vt_ref)


def _proj_fox_kernel(x_ref, wqt_ref, wk_ref, wvt_ref, wf_ref, bf_ref,
                     qt_ref, k_ref, vt_ref, lft_ref):
    xb = _proj_body(x_ref, wqt_ref, wk_ref, wvt_ref, qt_ref, k_ref, vt_ref)
    z = lax.dot_general(wf_ref[...], xb, _NT, preferred_element_type=F32) + bf_ref[...]
    lft_ref[...] = jnp.minimum(z, 0.0) - jnp.log1p(jnp.exp(-jnp.abs(z)))


def _project(h, w_qkv, wf_t=None, bf_col=None):
    r, d = h.shape
    tm = ROW_TILE
    wqt = w_qkv[:, :d].T.astype(BF16)
    wk = w_qkv[:, d:2 * d].astype(BF16)
    wvt = w_qkv[:, 2 * d:3 * d].T.astype(BF16)
    t_shape = jax.ShapeDtypeStruct((N_PAIRS, LANES, r), BF16)
    k_shape = jax.ShapeDtypeStruct((N_PAIRS, r, LANES), BF16)
    t_spec = pl.BlockSpec((N_PAIRS, LANES, tm), lambda i: (0, 0, i))
    k_spec = pl.BlockSpec((N_PAIRS, tm, LANES), lambda i: (0, i, 0))
    w_spec = pl.BlockSpec((d, d), lambda i: (0, 0))
    in_specs = [pl.BlockSpec((tm, d), lambda i: (i, 0)), w_spec, w_spec, w_spec]
    params = pltpu.CompilerParams(dimension_semantics=("arbitrary",),
                                  vmem_limit_bytes=VMEM_LIMIT)
    if wf_t is None:
        return pl.pallas_call(
            _proj_kernel, grid=(r // tm,), in_specs=in_specs,
            out_specs=[t_spec, k_spec, t_spec], out_shape=[t_shape, k_shape, t_shape],
            compiler_params=params, name="proj_diff")(h, wqt, wk, wvt)
    nh = wf_t.shape[0]
    in_specs += [pl.BlockSpec((nh, d), lambda i: (0, 0)),
                 pl.BlockSpec((nh, 1), lambda i: (0, 0))]
    return pl.pallas_call(
        _proj_fox_kernel, grid=(r // tm,), in_specs=in_specs,
        out_specs=[t_spec, k_spec, t_spec, pl.BlockSpec((nh, tm), lambda i: (0, i))],
        out_shape=[t_shape, k_shape, t_shape, jax.ShapeDtypeStruct((nh, r), F32)],
        compiler_params=params, name="proj_fox")(h, wqt, wk, wvt, wf_t, bf_col)


def _ck_lane(c):
    return HEAD_DIM if c == 0 else 0


def _placement_matrices(nh):
    e = np.zeros((nh, CK_PIECES * LANES, LANES), np.float32)
    for h in range(nh):
        for p in range(CK_PIECES):
            e[h, p * LANES + h, _ck_lane(h % 2) + p] = 1.0
    return e


def _cumsum_kernel(lfm_ref, lft_ref, k_ref, place_ref, cmeta_ref, kaug_ref, carry_ref):
    c = pl.program_id(1)
    nh, tc = lft_ref.shape

    @pl.when(c == 0)
    def _():
        lm = lfm_ref[0]
        n = lm.shape[1]
        tri = (lax.broadcasted_iota(jnp.int32, (n, n), 0)
               <= lax.broadcasted_iota(jnp.int32, (n, n), 1)).astype(F32)
        cm = jnp.dot(lm, tri, preferred_element_type=F32, precision=lax.Precision.HIGHEST)
        cmeta_ref[0] = cm
        carry_ref[...] = cm[:, n - 1:n]

    tri = (lax.broadcasted_iota(jnp.int32, (tc, tc), 0)
           <= lax.broadcasted_iota(jnp.int32, (tc, tc), 1)).astype(F32)
    cs = jnp.dot(lft_ref[...], tri, preferred_element_type=F32,
                 precision=lax.Precision.HIGHEST) + carry_ref[...]
    carry_ref[...] = cs[:, tc - 1:tc]

    cst = jnp.concatenate([cs, jnp.zeros((LANES - nh, tc), F32)], axis=0).T
    hi = cst.astype(BF16)
    rem = cst - hi.astype(F32)
    mid = rem.astype(BF16)
    lo = (rem - mid.astype(F32)).astype(BF16)
    pieces = jnp.concatenate([hi, mid, lo], axis=1)
    lane = lax.broadcasted_iota(jnp.int32, (tc, LANES), 1)
    for h in range(nh):
        ck = jnp.dot(pieces, place_ref[h], preferred_element_type=F32).astype(BF16)
        own = (lane < HEAD_DIM) if h % 2 == 0 else (lane >= HEAD_DIM)
        kaug_ref[h] = jnp.where(own, k_ref[h // 2], ck)


def _fox_cumsum(lft, k, bsz, seq):
    nh, _ = lft.shape
    rr = bsz * seq
    tc = ROW_TILE
    nc = seq // tc
    lf_meta = lft[:, rr:rr + bsz * N_META].reshape(nh, bsz, N_META).transpose(1, 0, 2)
    place = jnp.asarray(_placement_matrices(nh), BF16)
    return pl.pallas_call(
        _cumsum_kernel, grid=(bsz, nc),
        in_specs=[pl.BlockSpec((1, nh, N_META), lambda b, c: (b, 0, 0)),
                  pl.BlockSpec((nh, tc), lambda b, c: (0, b * nc + c)),
                  pl.BlockSpec((N_PAIRS, tc, LANES), lambda b, c: (0, b * nc + c, 0)),
                  pl.BlockSpec(place.shape, lambda b, c: (0, 0, 0))],
        out_specs=[pl.BlockSpec((1, nh, N_META), lambda b, c: (b, 0, 0)),
                   pl.BlockSpec((nh, tc, LANES), lambda b, c: (0, b * nc + c, 0))],
        out_shape=[jax.ShapeDtypeStruct((bsz, nh, N_META), F32),
                   jax.ShapeDtypeStruct((nh, rr, LANES), BF16)],
        scratch_shapes=[pltpu.VMEM((nh, 1), F32)],
        compiler_params=pltpu.CompilerParams(dimension_semantics=("arbitrary", "arbitrary"),
                                             vmem_limit_bytes=VMEM_LIMIT),
        name="fox_cumsum")(lf_meta, lft, k, place)


def _head_q(qt, c, fox):
    row = lax.broadcasted_iota(jnp.int32, qt.shape, 0)
    own = (row < HEAD_DIM) if c == 0 else (row >= HEAD_DIM)
    if fox:
        lo = _ck_lane(c)
        fill = jnp.where(jnp.logical_and(row >= lo, row < lo + CK_PIECES), -1.0, 0.0).astype(qt.dtype)
    else:
        fill = jnp.zeros_like(qt)
    return jnp.where(own, qt, fill)


def _head_k(kp, c):
    lane = lax.broadcasted_iota(jnp.int32, kp.shape, 1)
    own = (lane < HEAD_DIM) if c == 0 else (lane >= HEAD_DIM)
    return jnp.where(own, kp, jnp.zeros_like(kp))


def _diag_bias(d0, d1, n):
    zero = jnp.zeros_like(d0)
    neg = jnp.full_like(d0, NEG_INF)
    rows = []
    for r in range(n):
        blocks = []
        for c in range(n):
            if r > c:
                blocks.append(neg)
            elif r == c:
                blocks.append(d0)
            elif r == c - 1 and d1 is not None:
                blocks.append(d1)
            else:
                blocks.append(zero)
        rows.append(jnp.concatenate(blocks, axis=1))
    return jnp.concatenate(rows, axis=0)


def _causal_block():
    key = lax.broadcasted_iota(jnp.int32, (LANES, LANES), 0)
    qry = lax.broadcasted_iota(jnp.int32, (LANES, LANES), 1)
    return jnp.where(key > qry, NEG_INF, 0.0).astype(F32)


def _diff_lambda(lam_ref, lambda_init):
    lam = lam_ref[...]
    a = jnp.sum(lam[0:1] * lam[1:2], axis=1, keepdims=True)
    b = jnp.sum(lam[2:3] * lam[3:4], axis=1, keepdims=True)
    return jnp.exp(a) - jnp.exp(b) + lambda_init


def _diff_finish(acc0, l0, acc1, l1, lam_full, subln, lambda_init):
    o = acc0 / l0 - lam_full * (acc1 / l1)
    ms = jnp.mean(o * o, axis=0, keepdims=True)
    return o * lax.rsqrt(ms + LN_EPS) * subln * (1.0 - lambda_init)


def _fox_finish(acc0, l0, acc1, l1):
    row = lax.broadcasted_iota(jnp.int32, acc0.shape, 0)
    return jnp.where(row < HEAD_DIM, acc0 / l0, acc1 / l1)


def _attn_kernel(qmap_ref, kmap_ref, qt_ref, k_ref, vt_ref, km_ref, vmt_ref, *rest,
                 fox, lambda_init):
    if fox:
        ckm_ref, ot_ref = rest[:2]
    else:
        dtab_ref, mtab_ref, lam_ref, subln_ref, ot_ref = rest[:5]
    m_sc, l_sc, acc_sc = rest[-6:-4], rest[-4:-2], rest[-2:]
    t = pl.program_id(1)
    qi = qmap_ref[t]
    ki = kmap_ref[t]
    tq = qt_ref.shape[2]
    nblk = tq // LANES

    def init_from_meta(j, c, qa, near):
        mi = 2 * j + c
        km = km_ref[j]
        if fox:
            s = jnp.dot(_head_k(km, c), qa, preferred_element_type=F32) - ckm_ref[0, mi]
        else:
            s = jnp.dot(km, qa, preferred_element_type=F32)
            if near:
                s = s + jnp.concatenate(
                    [mtab_ref[mi], jnp.zeros((N_META, tq - LANES), F32)], axis=1)
        m = jnp.max(s, axis=0, keepdims=True)
        p = jnp.exp(s - m)
        m_sc[c][j] = m
        l_sc[c][j] = jnp.sum(p, axis=0, keepdims=True)
        acc_sc[c][j] = jnp.dot(vmt_ref[0, j], p.astype(BF16), preferred_element_type=F32)

    def tile_scores(j, c, qa, kind):
        mi = 2 * j + c
        kk = k_ref[mi] if fox else k_ref[j]
        s = jnp.dot(kk, qa, preferred_element_type=F32)
        if kind == "diag":
            if fox:
                s = s + _diag_bias(_causal_block(), None, nblk)
            else:
                s = s + _diag_bias(dtab_ref[mi, 0], dtab_ref[mi, 1], nblk)
        elif kind == "sub":
            bottom = jnp.concatenate(
                [dtab_ref[mi, 1], jnp.zeros((LANES, tq - LANES), F32)], axis=1)
            s = s + jnp.concatenate([jnp.zeros((tq - LANES, tq), F32), bottom], axis=0)
        return s

    def tile_probs(j, c, s):
        m_prev = m_sc[c][j]
        m_new = jnp.maximum(m_prev, jnp.max(s, axis=0, keepdims=True))
        alpha = jnp.exp(m_prev - m_new)
        p = jnp.exp(s - m_new)
        l_sc[c][j] = alpha * l_sc[c][j] + jnp.sum(p, axis=0, keepdims=True)
        m_sc[c][j] = m_new
        return alpha, p.astype(BF16)

    def tile_values(j, c, alpha, p):
        acc_sc[c][j] = alpha * acc_sc[c][j] + jnp.dot(vt_ref[j], p, preferred_element_type=F32)

    def run(kind, first, near_meta=False):
        def pair(j, carry):
            qt = qt_ref[j]
            qa = [_head_q(qt, c, fox) for c in range(2)]
            if first:
                for c in range(2):
                    init_from_meta(j, c, qa[c], near_meta)
            s = [tile_scores(j, c, qa[c], kind) for c in range(2)]
            ap = [tile_probs(j, c, s[c]) for c in range(2)]
            for c in range(2):
                tile_values(j, c, *ap[c])
            if kind == "diag":
                a0, a1 = acc_sc[0][j], acc_sc[1][j]
                l0, l1 = l_sc[0][j], l_sc[1][j]
                if fox:
                    o = _fox_finish(a0, l0, a1, l1)
                else:
                    o = _diff_finish(a0, l0, a1, l1, _diff_lambda(lam_ref, lambda_init),
                                     subln_ref[...], lambda_init)
                ot_ref[j] = o.astype(ot_ref.dtype)
            return carry
        lax.fori_loop(0, N_PAIRS, pair, 0, unroll=PAIR_UNROLL)

    @pl.when(jnp.logical_and(qi == 0, ki == 0))
    def _():
        run("diag", first=True, near_meta=True)

    @pl.when(jnp.logical_and(qi > 0, ki == 0))
    def _():
        if fox:
            run("far", first=True)
        else:
            @pl.when(qi == 1)
            def _():
                run("sub", first=True)

            @pl.when(qi > 1)
            def _():
                run("far", first=True)

    @pl.when(jnp.logical_and(ki > 0, ki == qi))
    def _():
        run("diag", first=False)

    @pl.when(jnp.logical_and(ki > 0, ki < qi))
    def _():
        if fox:
            run("far", first=False)
        else:
            @pl.when(ki == qi - 1)
            def _():
                run("sub", first=False)

            @pl.when(ki < qi - 1)
            def _():
                run("far", first=False)


def _attn_meta_kernel(qmt_ref, k_ref, vmt_ref, *rest, fox, lambda_init):
    if fox:
        ckm_ref, ot_ref = rest
    else:
        dtab_ref, lam_ref, subln_ref, ot_ref = rest
    causal = _causal_block()[:N_META, :N_META]

    def pair(j, carry):
        qt = qmt_ref[0, j]
        km = k_ref[j]
        acc, l = [], []
        for c in range(2):
            mi = 2 * j + c
            qa = _head_q(qt, c, fox)
            if fox:
                s = (jnp.dot(_head_k(km, c), qa, preferred_element_type=F32)
                     - ckm_ref[0, mi] + causal)
            else:
                s = jnp.dot(km, qa, preferred_element_type=F32) + dtab_ref[mi, 0][:N_META, :N_META]
            m = jnp.max(s, axis=0, keepdims=True)
            p = jnp.exp(s - m)
            l.append(jnp.sum(p, axis=0, keepdims=True))
            acc.append(jnp.dot(vmt_ref[0, j], p.astype(BF16), preferred_element_type=F32))
        if fox:
            o = _fox_finish(acc[0], l[0], acc[1], l[1])
        else:
            o = _diff_finish(acc[0], l[0], acc[1], l[1], _diff_lambda(lam_ref, lambda_init),
                             subln_ref[...][:, :N_META], lambda_init)
        ot_ref[0, j] = o.astype(ot_ref.dtype)
        return carry

    lax.fori_loop(0, N_PAIRS, pair, 0)


def _meta_columns(xt, bsz, rr):
    cols = xt[:, :, rr:rr + bsz * N_META]
    return cols.reshape(N_PAIRS, LANES, bsz, N_META).transpose(2, 0, 1, 3)


def _attention(qt, k, vt, bsz, seq, *, fox, lambda_init=0.0, extras=(), kaug=None):
    rr = bsz * seq
    r = qt.shape[2]
    tq = ATT_TILE
    nq = seq // tq
    qmap = np.concatenate([np.full(i + 1, i) for i in range(nq)]).astype(np.int32)
    kmap = np.concatenate([np.arange(i + 1) for i in range(nq)]).astype(np.int32)
    nsteps = int(qmap.shape[0])
    meta_blk = rr // N_META
    qmt = _meta_columns(qt, bsz, rr)
    vmt = _meta_columns(vt, bsz, rr)
    keys = kaug if fox else k
    nkh = keys.shape[0]

    tile_qt = pl.BlockSpec((N_PAIRS, LANES, tq), lambda b, t, qm, km: (0, 0, b * nq + qm[t]))
    tile_vt = pl.BlockSpec((N_PAIRS, LANES, tq), lambda b, t, qm, km: (0, 0, b * nq + km[t]))
    tile_k = pl.BlockSpec((nkh, tq, LANES), lambda b, t, qm, km: (0, b * nq + km[t], 0))
    meta_k = pl.BlockSpec((N_PAIRS, N_META, LANES), lambda b, t, qm, km: (0, meta_blk + b, 0))
    meta_vt = pl.BlockSpec((1, N_PAIRS, LANES, N_META), lambda b, t, qm, km: (b, 0, 0, 0))

    if fox:
        cmeta = extras[0]
        extra_specs = [pl.BlockSpec((1,) + cmeta.shape[1:], lambda b, t, qm, km: (b, 0, 0, 0))]
    else:
        extra_specs = [pl.BlockSpec(a.shape, functools.partial(lambda nd, b, t, qm, km: (0,) * nd,
                                                               a.ndim)) for a in extras]

    ot = pl.pallas_call(
        functools.partial(_attn_kernel, fox=fox, lambda_init=lambda_init),
        grid_spec=pltpu.PrefetchScalarGridSpec(
            num_scalar_prefetch=2, grid=(bsz, nsteps),
            in_specs=[tile_qt, tile_k, tile_vt, meta_k, meta_vt] + extra_specs,
            out_specs=tile_qt,
            scratch_shapes=[pltpu.VMEM((N_PAIRS, 1, tq), F32)] * 4
            + [pltpu.VMEM((N_PAIRS, LANES, tq), F32)] * 2),
        out_shape=jax.ShapeDtypeStruct((N_PAIRS, LANES, rr), BF16),
        compiler_params=pltpu.CompilerParams(
            dimension_semantics=("arbitrary", "arbitrary"), vmem_limit_bytes=VMEM_LIMIT),
        name="attn_fox" if fox else "attn_diff",
    )(jnp.asarray(qmap), jnp.asarray(kmap), qt, keys, vt, k, vmt, *extras)

    per_b = lambda b: (b, 0, 0, 0)
    if fox:
        m_extras = (extras[0],)
        m_specs = [pl.BlockSpec((1,) + extras[0].shape[1:], per_b)]
    else:
        m_extras = (extras[0], extras[2], extras[3])
        m_specs = [pl.BlockSpec(a.shape, functools.partial(lambda nd, b: (0,) * nd, a.ndim))
                   for a in m_extras]
    ot_meta = pl.pallas_call(
        functools.partial(_attn_meta_kernel, fox=fox, lambda_init=lambda_init),
        grid=(bsz,),
        in_specs=[pl.BlockSpec((1, N_PAIRS, LANES, N_META), per_b),
                  pl.BlockSpec((N_PAIRS, N_META, LANES), lambda b: (0, meta_blk + b, 0)),
                  pl.BlockSpec((1, N_PAIRS, LANES, N_META), per_b)] + m_specs,
        out_specs=pl.BlockSpec((1, N_PAIRS, LANES, N_META), per_b),
        out_shape=jax.ShapeDtypeStruct((bsz, N_PAIRS, LANES, N_META), BF16),
        compiler_params=pltpu.CompilerParams(dimension_semantics=("arbitrary",)),
        name="attn_fox_meta" if fox else "attn_diff_meta",
    )(qmt, k, vmt, *m_extras)
    tail = ot_meta.transpose(1, 2, 0, 3).reshape(N_PAIRS, LANES, bsz * N_META)
    tail = jnp.pad(tail, ((0, 0), (0, 0), (0, r - rr - bsz * N_META)))
    return ot, tail


def _layer_norm(y, g, b):
    mu = jnp.mean(y, axis=1, keepdims=True)
    yc = y - mu
    var = jnp.mean(yc * yc, axis=1, keepdims=True)
    return yc * lax.rsqrt(var + LN_EPS) * g + b


def _mix_router_kernel(ot_ref, otm_ref, h_ref, wo_ref, g_ref, b_ref, wr_ref, br_ref,
                       h1_ref, route_ref, counts_ref, carry_ref):
    i = pl.program_id(0)
    tm = h_ref.shape[0]

    @pl.when(i == 0)
    def _():
        carry_ref[...] = jnp.zeros_like(carry_ref)

    is_tail = i == pl.num_programs(0) - 1
    ot = jnp.concatenate([jnp.where(is_tail, otm_ref[j], ot_ref[j]) for j in range(N_PAIRS)],
                         axis=0)
    mix = lax.dot_general(ot, wo_ref[...], _TN, preferred_element_type=F32)
    h1 = _layer_norm(DN_ALPHA * h_ref[...] + mix, g_ref[...], b_ref[...])
    h1_ref[...] = h1

    logits = jnp.dot(h1.astype(BF16), wr_ref[...], preferred_element_type=F32) + br_ref[...]
    lane = lax.broadcasted_iota(jnp.int32, logits.shape, 1)
    lane_f = lane.astype(F32)
    big = float(LANES)
    g_mask = lane < N_GROUPS
    lg = jnp.where(g_mask, logits, NEG_INF)
    gmax = jnp.max(lg, axis=1, keepdims=True)
    gsel = jnp.min(jnp.where(lg == gmax, lane_f, big), axis=1, keepdims=True)
    p_group = 1.0 / jnp.sum(jnp.where(g_mask, jnp.exp(lg - gmax), 0.0), axis=1, keepdims=True)
    e_lo = N_GROUPS + EXPERTS_PER_GROUP * gsel
    e_mask = jnp.logical_and(lane_f >= e_lo, lane_f < e_lo + EXPERTS_PER_GROUP)
    le = jnp.where(e_mask, logits, NEG_INF)
    v1 = jnp.max(le, axis=1, keepdims=True)
    i1 = jnp.min(jnp.where(jnp.logical_and(le == v1, e_mask), lane_f, big),
                 axis=1, keepdims=True)
    e_mask2 = jnp.logical_and(e_mask, lane_f != i1)
    le2 = jnp.where(e_mask2, logits, NEG_INF)
    v2 = jnp.max(le2, axis=1, keepdims=True)
    i2 = jnp.min(jnp.where(jnp.logical_and(le2 == v2, e_mask2), lane_f, big),
                 axis=1, keepdims=True)
    ex = jnp.exp(v2 - v1)
    gate0 = p_group * (1.0 / (1.0 + ex))
    gate1 = p_group * (ex / (1.0 + ex))
    e0 = i1 - N_GROUPS
    e1 = i2 - N_GROUPS

    onehot = jnp.logical_or(lane_f == e0, lane_f == e1)
    oh_bf = jnp.where(onehot, 1.0, 0.0).astype(BF16)
    ltri = jnp.where(lax.broadcasted_iota(jnp.int32, (tm, tm), 1)
                     < lax.broadcasted_iota(jnp.int32, (tm, tm), 0), 1.0, 0.0).astype(BF16)
    before = jnp.dot(ltri, oh_bf, preferred_element_type=F32) + carry_ref[...]
    r0 = jnp.sum(jnp.where(lane_f == e0, before, 0.0), axis=1, keepdims=True)
    r1 = jnp.sum(jnp.where(lane_f == e1, before, 0.0), axis=1, keepdims=True)
    carry = carry_ref[...] + jnp.sum(jnp.where(onehot, 1.0, 0.0), axis=0, keepdims=True)
    carry_ref[...] = carry
    counts_ref[...] = carry

    route = jnp.zeros_like(logits)
    for col, val in enumerate((e0, e1, r0, r1, gate0, gate1)):
        route = jnp.where(lane == col, val, route)
    route_ref[...] = route


def _mix_and_route(ot, ot_tail, h, wo_bf, ln_g, ln_b, wr_bf, br):
    r, d = h.shape
    tm = ROW_TILE
    assert ot_tail.shape[2] == tm and ot.shape[2] == r - tm
    last_real = ot.shape[2] // tm - 1
    row = lambda i: (i, 0)
    const = lambda i: (0, 0)
    return pl.pallas_call(
        _mix_router_kernel, grid=(r // tm,),
        in_specs=[pl.BlockSpec((N_PAIRS, LANES, tm), lambda i: (0, 0, jnp.minimum(i, last_real))),
                  pl.BlockSpec((N_PAIRS, LANES, tm), lambda i: (0, 0, 0)),
                  pl.BlockSpec((tm, d), row),
                  pl.BlockSpec((d, d), const),
                  pl.BlockSpec((1, d), const), pl.BlockSpec((1, d), const),
                  pl.BlockSpec((d, LANES), const), pl.BlockSpec((1, LANES), const)],
        out_specs=[pl.BlockSpec((tm, d), row), pl.BlockSpec((tm, LANES), row),
                   pl.BlockSpec((1, LANES), const)],
        out_shape=[jax.ShapeDtypeStruct((r, d), F32), jax.ShapeDtypeStruct((r, LANES), F32),
                   jax.ShapeDtypeStruct((1, LANES), F32)],
        scratch_shapes=[pltpu.VMEM((1, LANES), F32)],
        compiler_params=pltpu.CompilerParams(dimension_semantics=("arbitrary",),
                                             vmem_limit_bytes=VMEM_LIMIT),
        name="mix_router")(ot, ot_tail, h, wo_bf, ln_g, ln_b, wr_bf, br)


def _row_copy(src_ref, src_row, dst_ref, dst_row, sem):
    return pltpu.make_async_copy(src_ref.at[pl.ds(src_row, 1), :],
                                 dst_ref.at[pl.ds(dst_row, 1), :], sem)


def _dispatch_kernel(slots_ref, h_ref, xs_in_ref, xs_ref, slot_smem, sem, idx_sem):
    del xs_in_ref
    i = pl.program_id(0)
    tm = h_ref.shape[0]
    idx_copy = pltpu.make_async_copy(slots_ref.at[i], slot_smem, idx_sem)
    idx_copy.start()
    idx_copy.wait()

    def start(t, carry):
        for c in range(2):
            _row_copy(h_ref, t, xs_ref, slot_smem[c * tm + t], sem).start()
        return carry

    def wait(t, carry):
        for c in range(2):
            _row_copy(h_ref, t, xs_ref, slot_smem[c * tm + t], sem).wait()
        return carry

    lax.fori_loop(0, tm, start, 0)
    lax.fori_loop(0, tm, wait, 0)


def _dispatch(h1, slots, n_rows):
    r, d = h1.shape
    tm = ROW_TILE
    xs0 = jnp.zeros((n_rows, d), F32)
    any_spec = pl.BlockSpec(memory_space=pl.ANY)
    return pl.pallas_call(
        _dispatch_kernel, grid=(r // tm,),
        in_specs=[any_spec, pl.BlockSpec((tm, d), lambda i: (i, 0)), any_spec],
        out_specs=any_spec,
        out_shape=jax.ShapeDtypeStruct((n_rows, d), F32),
        scratch_shapes=[pltpu.SMEM((2 * tm,), jnp.int32), pltpu.SemaphoreType.DMA,
                        pltpu.SemaphoreType.DMA],
        input_output_aliases={2: 0},
        compiler_params=pltpu.CompilerParams(dimension_semantics=("arbitrary",)),
        name="moe_dispatch")(slots, h1, xs0)


def _expert_kernel(tile_e_ref, n_used_ref, x_ref, wg_ref, wu_ref, wd_ref, y_ref):
    del tile_e_ref

    @pl.when(pl.program_id(0) < n_used_ref[0])
    def _():
        xb = x_ref[...].astype(BF16)
        g = jnp.dot(xb, wg_ref[0], preferred_element_type=F32)
        u = jnp.dot(xb, wu_ref[0], preferred_element_type=F32)
        a = (g * jax.nn.sigmoid(g) * u).astype(BF16)
        y_ref[...] = jnp.dot(a, wd_ref[0], preferred_element_type=F32)

    @pl.when(pl.program_id(0) >= n_used_ref[0])
    def _():
        y_ref[...] = jnp.zeros_like(y_ref)


def _experts(xs, tile_e, n_used, wg_bf, wu_bf, wd_bf):
    p, d = xs.shape
    te = EXP_TILE
    de = wg_bf.shape[2]
    row = lambda i, te_ref, nu_ref: (jnp.minimum(i, nu_ref[0] - 1), 0)
    wsel = lambda i, te_ref, nu_ref: (te_ref[i], 0, 0)
    return pl.pallas_call(
        _expert_kernel,
        grid_spec=pltpu.PrefetchScalarGridSpec(
            num_scalar_prefetch=2, grid=(p // te,),
            in_specs=[pl.BlockSpec((te, d), row),
                      pl.BlockSpec((1, d, de), wsel), pl.BlockSpec((1, d, de), wsel),
                      pl.BlockSpec((1, de, d), wsel)],
            out_specs=pl.BlockSpec((te, d), lambda i, te_ref, nu_ref: (i, 0))),
        out_shape=jax.ShapeDtypeStruct((p, d), F32),
        compiler_params=pltpu.CompilerParams(dimension_semantics=("arbitrary",),
                                             vmem_limit_bytes=VMEM_LIMIT),
        name="moe_experts")(tile_e, n_used, xs, wg_bf, wu_bf, wd_bf)


def _combine_kernel(slots_ref, ys_ref, route_ref, h_ref, g_ref, b_ref, out_ref,
                    slot_smem, y0_ref, y1_ref, sem, idx_sem):
    i = pl.program_id(0)
    tm = h_ref.shape[0]
    idx_copy = pltpu.make_async_copy(slots_ref.at[i], slot_smem, idx_sem)
    idx_copy.start()
    idx_copy.wait()
    bufs = (y0_ref, y1_ref)

    def start(t, carry):
        for c in range(2):
            _row_copy(ys_ref, slot_smem[c * tm + t], bufs[c], t, sem).start()
        return carry

    def wait(t, carry):
        for c in range(2):
            _row_copy(ys_ref, slot_smem[c * tm + t], bufs[c], t, sem).wait()
        return carry

    lax.fori_loop(0, tm, start, 0)
    lax.fori_loop(0, tm, wait, 0)
    route = route_ref[...]
    ffn = route[:, 4:5] * y0_ref[...] + route[:, 5:6] * y1_ref[...]
    out_ref[...] = _layer_norm(DN_ALPHA * h_ref[...] + ffn, g_ref[...], b_ref[...])


def _combine(ys, slots, route, h1, ln_g, ln_b, n_out_rows):
    r, d = h1.shape
    tm = ROW_TILE
    row = lambda i: (i, 0)
    const = lambda i: (0, 0)
    any_spec = pl.BlockSpec(memory_space=pl.ANY)
    return pl.pallas_call(
        _combine_kernel, grid=(n_out_rows // tm,),
        in_specs=[any_spec, any_spec, pl.BlockSpec((tm, LANES), row),
                  pl.BlockSpec((tm, d), row), pl.BlockSpec((1, d), const),
                  pl.BlockSpec((1, d), const)],
        out_specs=pl.BlockSpec((tm, d), row),
        out_shape=jax.ShapeDtypeStruct((n_out_rows, d), F32),
        scratch_shapes=[pltpu.SMEM((2 * tm,), jnp.int32), pltpu.VMEM((tm, d), F32),
                        pltpu.VMEM((tm, d), F32), pltpu.SemaphoreType.DMA,
                        pltpu.SemaphoreType.DMA],
        compiler_params=pltpu.CompilerParams(dimension_semantics=("arbitrary",),
                                             vmem_limit_bytes=VMEM_LIMIT),
        name="moe_combine")(slots, ys, route, h1, ln_g, ln_b)


def _moe(h1, route, counts, wg_bf, wu_bf, wd_bf, ln_g, ln_b, n_out_rows):
    r, _ = h1.shape
    tm, te = ROW_TILE, EXP_TILE
    n_tiles = (2 * r) // te + N_EXPERTS
    cnt = counts[0, :N_EXPERTS].astype(jnp.int32)
    seg_tiles = (cnt + te - 1) // te
    seg_end = jnp.cumsum(seg_tiles)
    seg_start = (seg_end - seg_tiles) * te
    n_used = seg_end[-1:]
    tile_ids = jnp.arange(n_tiles, dtype=jnp.int32)
    tile_e = jnp.sum(tile_ids[:, None] >= seg_end[None, :], axis=1)
    last_e = jnp.sum(n_used - 1 >= seg_end)
    tile_e = jnp.minimum(tile_e, last_e).astype(jnp.int32)
    e01 = route[:, 0:2].astype(jnp.int32)
    slot = seg_start[e01] + route[:, 2:4].astype(jnp.int32)
    slots = slot.reshape(r // tm, tm, 2).transpose(0, 2, 1).reshape(r // tm, 2 * tm)

    xs = _dispatch(h1, slots, n_tiles * te)
    ys = _experts(xs, tile_e, n_used.astype(jnp.int32), wg_bf, wu_bf, wd_bf)
    return _combine(ys, slots, route, h1, ln_g, ln_b, n_out_rows)


def _diff_tables(rel_bias):
    bkt = _bucket_table(2 * LANES + N_META)
    rel = (rel_bias - rel_bias[N_BUCKETS - 1:N_BUCKETS]).T
    key = np.arange(LANES)[:, None]
    qry = np.arange(LANES)[None, :]
    d0 = jnp.where(jnp.asarray(qry >= key), rel[:, bkt[np.maximum(qry - key, 0)]], NEG_INF)
    d1 = rel[:, bkt[LANES + qry - key]]
    dtab = jnp.stack([d0, d1], axis=1)
    km = np.arange(N_META)[:, None]
    mtab = rel[:, bkt[N_META + qry - km]]
    return dtab.astype(F32), mtab.astype(F32)


def kernel(x, meta_tokens, rel_bias, diff_w_qkv, diff_lambda, diff_subln, diff_w_o,
           fox_w_in, fox_b_f, fox_w_o, ln_mix_g, ln_mix_b, ln_ffn_g, ln_ffn_b,
           router_group_w, router_group_b, router_expert_w, router_expert_b,
           expert_w_gate, expert_w_up, expert_w_down):
    bsz, seq, d = x.shape
    rr = bsz * seq
    r = rr + ROW_TILE
    assert seq % ATT_TILE == 0 and bsz * N_META <= ROW_TILE and d == N_PAIRS * LANES
    n_pad = r - rr - bsz * N_META
    h = jnp.concatenate([x.reshape(rr, d),
                         jnp.tile(meta_tokens.astype(x.dtype), (bsz, 1)),
                         jnp.zeros((n_pad, d), x.dtype)], axis=0)

    for i in range(DEPTH):
        j = i // 2
        if i % 2 == 0:
            lam0 = _lambda_init(i)
            qt, k, vt = _project(h, diff_w_qkv[j])
            dtab, mtab = _diff_tables(rel_bias)
            subln = jnp.broadcast_to(diff_subln[j].astype(F32)[:, None], (2 * HEAD_DIM, ATT_TILE))
            ot, ot_tail = _attention(qt, k, vt, bsz, seq, fox=False, lambda_init=lam0,
                                     extras=(dtab, mtab, diff_lambda[j].astype(F32), subln))
            w_o = diff_w_o[j]
        else:
            w_in = fox_w_in[j]
            qt, k, vt, lft = _project(h, w_in[:, :3 * d], w_in[:, 3 * d:].T.astype(BF16),
                                      fox_b_f[j].astype(F32)[:, None])
            cmeta, kaug = _fox_cumsum(lft, k, bsz, seq)
            ot, ot_tail = _attention(qt, k, vt, bsz, seq, fox=True,
                                     extras=(cmeta[..., None],), kaug=kaug)
            w_o = fox_w_o[j]

        n_router = N_GROUPS + N_EXPERTS
        wr = jnp.concatenate([router_group_w[i], router_expert_w[i].reshape(d, N_EXPERTS)], axis=1)
        wr = jnp.pad(wr, ((0, 0), (0, LANES - n_router))).astype(BF16)
        br = jnp.concatenate([router_group_b[i], router_expert_b[i].reshape(N_EXPERTS)])
        br = jnp.pad(br, (0, LANES - n_router)).astype(F32)[None, :]
        h1, route, counts = _mix_and_route(ot, ot_tail, h, w_o.astype(BF16), ln_mix_g[i][None, :],
                                           ln_mix_b[i][None, :], wr, br)
        n_out = rr if i == DEPTH - 1 else r
        h = _moe(h1, route, counts, expert_w_gate[i].astype(BF16), expert_w_up[i].astype(BF16),
                 expert_w_down[i].astype(BF16), ln_ffn_g[i][None, :], ln_ffn_b[i][None, :], n_out)

    return h.reshape(bsz, seq, d)
```

```python
import functools
import math

import numpy as np
import jax
import jax.numpy as jnp
from jax import lax
from jax.experimental import pallas as pl
from jax.experimental.pallas import tpu as pltpu

F32 = jnp.float32
BF16 = jnp.bfloat16

N_META = 16
HEAD_DIM = 64
N_PAIRS = 8
LANES = 128
N_BUCKETS = 32
MAX_EXACT = 16
MAX_DISTANCE = 128
N_GROUPS = 4
EXPERTS_PER_GROUP = 8
N_EXPERTS = 32
DEPTH = 2
DN_ALPHA = (2 * DEPTH) ** 0.25
LN_EPS = 1e-5
NEG_INF = -1e30
CK_PIECES = 3
LOG2E = math.log2(math.e)
ONES_ROWS = 16

ROW_TILE = 512
ATT_TILE = 512
PAIR_GROUP = 2
EXP_TILE = 512
DMA_UNROLL = 8
VMEM_LIMIT = 48 * 1024 * 1024

_NT = (((1,), (1,)), ((), ()))
_TN = (((0,), (0,)), ((), ()))


def _lambda_init(layer_idx):
    return 0.8 - 0.6 * math.exp(-0.3 * layer_idx)


def _bucket_table(n):
    d = np.arange(n)
    nf = np.maximum(d, 1).astype(np.float64)
    large = MAX_EXACT + (np.log(nf / MAX_EXACT) / math.log(MAX_DISTANCE / MAX_EXACT)
                         * (N_BUCKETS - MAX_EXACT)).astype(np.int64)
    large = np.minimum(large, N_BUCKETS - 1)
    return np.where(d < MAX_EXACT, d, large).astype(np.int32)


def _proj_body(x_ref, wqt_ref, wk_ref, wvt_ref, qt_ref, k_ref, vt_ref):
    xb = x_ref[...].astype(BF16)
    qt = lax.dot_general(wqt_ref[...], xb, _NT, preferred_element_type=F32) * (HEAD_DIM ** -0.5 * LOG2E)
    kk = jnp.dot(xb, wk_ref[...], preferred_element_type=F32)
    vt = lax.dot_general(wvt_ref[...], xb, _NT, preferred_element_type=F32)
    for j in range(N_PAIRS):
        sl = slice(j * LANES, (j + 1) * LANES)
        qt_ref[j] = qt[sl, :].astype(BF16)
        k_ref[j] = kk[:, sl].astype(BF16)
        vt_ref[j] = vt[sl, :].astype(BF16)
    return xb


def _proj_kernel(x_ref, wqt_ref, wk_ref, wvt_ref, qt_ref, k_ref, vt_ref):
    _proj_body(x_ref, wqt_ref, wk_ref, wvt_ref, qt_ref, k_ref, vt_ref)


def _proj_fox_kernel(x_ref, wqt_ref, wk_ref, wvt_ref, wf_ref, bf_ref,
                     qt_ref, k_ref, vt_ref, lft_ref):
    xb = _proj_body(x_ref, wqt_ref, wk_ref, wvt_ref, qt_ref, k_ref, vt_ref)
    z = lax.dot_general(wf_ref[...], xb, _NT, preferred_element_type=F32) + bf_ref[...]
    lft_ref[...] = jnp.minimum(z, 0.0) - jnp.log1p(jnp.exp(-jnp.abs(z)))


def _project(h, w_qkv, wf_t=None, bf_col=None):
    r, d = h.shape
    tm = ROW_TILE
    wqt = w_qkv[:, :d].T.astype(BF16)
    wk = w_qkv[:, d:2 * d].astype(BF16)
    wvt = w_qkv[:, 2 * d:3 * d].T.astype(BF16)
    t_shape = jax.ShapeDtypeStruct((N_PAIRS, LANES, r), BF16)
    k_shape = jax.ShapeDtypeStruct((N_PAIRS, r, LANES), BF16)
    t_spec = pl.BlockSpec((N_PAIRS, LANES, tm), lambda i: (0, 0, i))
    k_spec = pl.BlockSpec((N_PAIRS, tm, LANES), lambda i: (0, i, 0))
    w_spec = pl.BlockSpec((d, d), lambda i: (0, 0))
    in_specs = [pl.BlockSpec((tm, d), lambda i: (i, 0)), w_spec, w_spec, w_spec]
    params = pltpu.CompilerParams(dimension_semantics=("arbitrary",),
                                  vmem_limit_bytes=VMEM_LIMIT)
    if wf_t is None:
        return pl.pallas_call(
            _proj_kernel, grid=(r // tm,), in_specs=in_specs,
            out_specs=[t_spec, k_spec, t_spec], out_shape=[t_shape, k_shape, t_shape],
            compiler_params=params, name="proj_diff")(h, wqt, wk, wvt)
    nh = wf_t.shape[0]
    in_specs += [pl.BlockSpec((nh, d), lambda i: (0, 0)),
                 pl.BlockSpec((nh, 1), lambda i: (0, 0))]
    return pl.pallas_call(
        _proj_fox_kernel, grid=(r // tm,), in_specs=in_specs,
        out_specs=[t_spec, k_spec, t_spec, pl.BlockSpec((nh, tm), lambda i: (0, i))],
        out_shape=[t_shape, k_shape, t_shape, jax.ShapeDtypeStruct((nh, r), F32)],
        compiler_params=params, name="proj_fox")(h, wqt, wk, wvt, wf_t, bf_col)


def _ck_lane(c):
    return HEAD_DIM if c == 0 else 0


def _placement_matrices(nh):
    e = np.zeros((nh, CK_PIECES * LANES, LANES), np.float32)
    for h in range(nh):
        for p in range(CK_PIECES):
            e[h, p * LANES + h, _ck_lane(h % 2) + p] = 1.0
    return e


def _cumsum_kernel(lfm_ref, lft_ref, k_ref, place_ref, cmeta_ref, kaug_ref, carry_ref):
    c = pl.program_id(1)
    nh, tc = lft_ref.shape

    @pl.when(c == 0)
    def _():
        lm = lfm_ref[0]
        n = lm.shape[1]
        tri = (lax.broadcasted_iota(jnp.int32, (n, n), 0)
               <= lax.broadcasted_iota(jnp.int32, (n, n), 1)).astype(F32)
        cm = jnp.dot(lm, tri, preferred_element_type=F32, precision=lax.Precision.HIGHEST)
        cmeta_ref[0] = cm * LOG2E
        carry_ref[...] = cm[:, n - 1:n]

    tri = (lax.broadcasted_iota(jnp.int32, (tc, tc), 0)
           <= lax.broadcasted_iota(jnp.int32, (tc, tc), 1)).astype(F32)
    cs = jnp.dot(lft_ref[...], tri, preferred_element_type=F32,
                 precision=lax.Precision.HIGHEST) + carry_ref[...]
    carry_ref[...] = cs[:, tc - 1:tc]

    cst = jnp.concatenate([cs * LOG2E, jnp.zeros((LANES - nh, tc), F32)], axis=0).T
    hi = cst.astype(BF16)
    rem = cst - hi.astype(F32)
    mid = rem.astype(BF16)
    lo = (rem - mid.astype(F32)).astype(BF16)
    pieces = jnp.concatenate([hi, mid, lo], axis=1)
    lane = lax.broadcasted_iota(jnp.int32, (tc, LANES), 1)
    for h in range(nh):
        ck = jnp.dot(pieces, place_ref[h], preferred_element_type=F32).astype(BF16)
        own = (lane < HEAD_DIM) if h % 2 == 0 else (lane >= HEAD_DIM)
        kaug_ref[h] = jnp.where(own, k_ref[h // 2], ck)


def _fox_cumsum(lft, k, bsz, seq):
    nh, _ = lft.shape
    rr = bsz * seq
    tc = ROW_TILE
    nc = seq // tc
    lf_meta = lft[:, rr:rr + bsz * N_META].reshape(nh, bsz, N_META).transpose(1, 0, 2)
    place = jnp.asarray(_placement_matrices(nh), BF16)
    return pl.pallas_call(
        _cumsum_kernel, grid=(bsz, nc),
        in_specs=[pl.BlockSpec((1, nh, N_META), lambda b, c: (b, 0, 0)),
                  pl.BlockSpec((nh, tc), lambda b, c: (0, b * nc + c)),
                  pl.BlockSpec((N_PAIRS, tc, LANES), lambda b, c: (0, b * nc + c, 0)),
                  pl.BlockSpec(place.shape, lambda b, c: (0, 0, 0))],
        out_specs=[pl.BlockSpec((1, nh, N_META), lambda b, c: (b, 0, 0)),
                   pl.BlockSpec((nh, tc, LANES), lambda b, c: (0, b * nc + c, 0))],
        out_shape=[jax.ShapeDtypeStruct((bsz, nh, N_META), F32),
                   jax.ShapeDtypeStruct((nh, rr, LANES), BF16)],
        scratch_shapes=[pltpu.VMEM((nh, 1), F32)],
        compiler_params=pltpu.CompilerParams(dimension_semantics=("arbitrary", "arbitrary"),
                                             vmem_limit_bytes=VMEM_LIMIT),
        name="fox_cumsum")(lf_meta, lft, k, place)


def _head_q(qt, c, fox):
    row = lax.broadcasted_iota(jnp.int32, qt.shape, 0)
    own = (row < HEAD_DIM) if c == 0 else (row >= HEAD_DIM)
    if fox:
        lo = _ck_lane(c)
        fill = jnp.where(jnp.logical_and(row >= lo, row < lo + CK_PIECES), -1.0, 0.0).astype(qt.dtype)
    else:
        fill = jnp.zeros_like(qt)
    return jnp.where(own, qt, fill)


def _head_k(kp, c):
    lane = lax.broadcasted_iota(jnp.int32, kp.shape, 1)
    own = (lane < HEAD_DIM) if c == 0 else (lane >= HEAD_DIM)
    return jnp.where(own, kp, jnp.zeros_like(kp))


def _diag_bias(d0, d1, n):
    zero = jnp.zeros_like(d0)
    neg = jnp.full_like(d0, NEG_INF)
    rows = []
    for r in range(n):
        blocks = []
        for c in range(n):
            if r > c:
                blocks.append(neg)
            elif r == c:
                blocks.append(d0)
            elif r == c - 1 and d1 is not None:
                blocks.append(d1)
            else:
                blocks.append(zero)
        rows.append(jnp.concatenate(blocks, axis=1))
    return jnp.concatenate(rows, axis=0)


def _causal_block():
    key = lax.broadcasted_iota(jnp.int32, (LANES, LANES), 0)
    qry = lax.broadcasted_iota(jnp.int32, (LANES, LANES), 1)
    return jnp.where(key > qry, NEG_INF, 0.0).astype(F32)


def _diff_lambda(lam_ref, lambda_init):
    lam = lam_ref[...]
    a = jnp.sum(lam[0:1] * lam[1:2], axis=1, keepdims=True)
    b = jnp.sum(lam[2:3] * lam[3:4], axis=1, keepdims=True)
    return jnp.exp(a) - jnp.exp(b) + lambda_init


def _head_v(vt, c, fox):
    if fox:
        row = lax.broadcasted_iota(jnp.int32, vt.shape, 0)
        own = (row < HEAD_DIM) if c == 0 else (row >= HEAD_DIM)
        return jnp.where(own, vt, jnp.ones_like(vt))
    return jnp.concatenate([vt, jnp.ones((ONES_ROWS, vt.shape[1]), vt.dtype)], axis=0)


def _denominator(acc, c, fox):
    r = _ck_lane(c) if fox else 2 * HEAD_DIM
    return acc[r:r + 1, :]


def _diff_finish(acc0, acc1, lam_full, subln, lambda_init):
    nv = 2 * HEAD_DIM
    o = (acc0[:nv] / _denominator(acc0, 0, False)
         - lam_full * (acc1[:nv] / _denominator(acc1, 1, False)))
    ms = jnp.mean(o * o, axis=0, keepdims=True)
    return o * lax.rsqrt(ms + LN_EPS) * subln * (1.0 - lambda_init)


def _fox_finish(acc0, acc1):
    row = lax.broadcasted_iota(jnp.int32, acc0.shape, 0)
    return jnp.where(row < HEAD_DIM, acc0 / _denominator(acc0, 0, True),
                     acc1 / _denominator(acc1, 1, True))


def _attn_kernel(qmap_ref, kmap_ref, qt_ref, k_ref, vt_ref, km_ref, vmt_ref, *rest,
                 fox, lambda_init):
    if fox:
        ckm_ref, ot_ref = rest[:2]
    else:
        dtab_ref, mtab_ref, lam_ref, subln_ref, ot_ref = rest[:5]
    m_sc, acc_sc = rest[-4:-2], rest[-2:]
    t = pl.program_id(1)
    qi = qmap_ref[t]
    ki = kmap_ref[t]
    tq = qt_ref.shape[2]
    nblk = tq // LANES

    def init_from_meta(j, c, qa, near):
        mi = 2 * j + c
        km = km_ref[j]
        if fox:
            s = jnp.dot(_head_k(km, c), qa, preferred_element_type=F32) - ckm_ref[0, mi]
        else:
            s = jnp.dot(km, qa, preferred_element_type=F32)
            if near:
                s = s + jnp.concatenate(
                    [mtab_ref[mi], jnp.zeros((N_META, tq - LANES), F32)], axis=1)
        m = jnp.max(s, axis=0, keepdims=True)
        p = jnp.exp2(s - m)
        m_sc[c][j] = m
        acc_sc[c][j] = jnp.dot(_head_v(vmt_ref[0, j], c, fox), p.astype(BF16),
                               preferred_element_type=F32)

    def tile_scores(j, c, qa, kind):
        mi = 2 * j + c
        kk = k_ref[mi] if fox else k_ref[j]
        s = jnp.dot(kk, qa, preferred_element_type=F32)
        if kind == "diag":
            if fox:
                s = s + _diag_bias(_causal_block(), None, nblk)
            else:
                s = s + _diag_bias(dtab_ref[mi, 0], dtab_ref[mi, 1], nblk)
        elif kind == "sub":
            bottom = jnp.concatenate(
                [dtab_ref[mi, 1], jnp.zeros((LANES, tq - LANES), F32)], axis=1)
            s = s + jnp.concatenate([jnp.zeros((tq - LANES, tq), F32), bottom], axis=0)
        return s

    def tile_probs(j, c, s):
        m_prev = m_sc[c][j]
        m_new = jnp.maximum(m_prev, jnp.max(s, axis=0, keepdims=True))
        alpha = jnp.exp2(m_prev - m_new)
        p = jnp.exp2(s - m_new)
        m_sc[c][j] = m_new
        return alpha, p.astype(BF16)

    def tile_values(j, c, alpha, p):
        acc_sc[c][j] = alpha * acc_sc[c][j] + jnp.dot(_head_v(vt_ref[j], c, fox), p,
                                                      preferred_element_type=F32)

    def run(kind, first, near_meta=False):
        def group(g, carry):
            pairs = [g * PAIR_GROUP + u for u in range(PAIR_GROUP)]
            scores = []
            for j in pairs:
                qt = qt_ref[j]
                qa = [_head_q(qt, c, fox) for c in range(2)]
                if first:
                    for c in range(2):
                        init_from_meta(j, c, qa[c], near_meta)
                scores.append([tile_scores(j, c, qa[c], kind) for c in range(2)])
            for j, s in zip(pairs, scores):
                ap = [tile_probs(j, c, s[c]) for c in range(2)]
                for c in range(2):
                    tile_values(j, c, *ap[c])
                if kind == "diag":
                    if fox:
                        o = _fox_finish(acc_sc[0][j], acc_sc[1][j])
                    else:
                        o = _diff_finish(acc_sc[0][j], acc_sc[1][j],
                                         _diff_lambda(lam_ref, lambda_init),
                                         subln_ref[...], lambda_init)
                    ot_ref[j] = o.astype(ot_ref.dtype)
            return carry
        lax.fori_loop(0, N_PAIRS // PAIR_GROUP, group, 0)

    @pl.when(jnp.logical_and(qi == 0, ki == 0))
    def _():
        run("diag", first=True, near_meta=True)

    @pl.when(jnp.logical_and(qi > 0, ki == 0))
    def _():
        if fox:
            run("far", first=True)
        else:
            @pl.when(qi == 1)
            def _():
                run("sub", first=True)

            @pl.when(qi > 1)
            def _():
                run("far", first=True)

    @pl.when(jnp.logical_and(ki > 0, ki == qi))
    def _():
        run("diag", first=False)

    @pl.when(jnp.logical_and(ki > 0, ki < qi))
    def _():
        if fox:
            run("far", first=False)
        else:
            @pl.when(ki == qi - 1)
            def _():
                run("sub", first=False)

            @pl.when(ki < qi - 1)
            def _():
                run("far", first=False)


def _attn_meta_kernel(qmt_ref, k_ref, vmt_ref, *rest, fox, lambda_init):
    if fox:
        ckm_ref, ot_ref = rest
    else:
        dtab_ref, lam_ref, subln_ref, ot_ref = rest
    causal = _causal_block()[:N_META, :N_META]

    def pair(j, carry):
        qt = qmt_ref[0, j]
        km = k_ref[j]
        acc = []
        for c in range(2):
            mi = 2 * j + c
            qa = _head_q(qt, c, fox)
            if fox:
                s = (jnp.dot(_head_k(km, c), qa, preferred_element_type=F32)
                     - ckm_ref[0, mi] + causal)
            else:
                s = jnp.dot(km, qa, preferred_element_type=F32) + dtab_ref[mi, 0][:N_META, :N_META]
            m = jnp.max(s, axis=0, keepdims=True)
            p = jnp.exp2(s - m)
            acc.append(jnp.dot(_head_v(vmt_ref[0, j], c, fox), p.astype(BF16),
                               preferred_element_type=F32))
        if fox:
            o = _fox_finish(acc[0], acc[1])
        else:
            o = _diff_finish(acc[0], acc[1], _diff_lambda(lam_ref, lambda_init),
                             subln_ref[...][:, :N_META], lambda_init)
        ot_ref[0, j] = o.astype(ot_ref.dtype)
        return carry

    lax.fori_loop(0, N_PAIRS, pair, 0)


def _meta_columns(xt, bsz, rr):
    cols = xt[:, :, rr:rr + bsz * N_META]
    return cols.reshape(N_PAIRS, LANES, bsz, N_META).transpose(2, 0, 1, 3)


def _attention(qt, k, vt, bsz, seq, *, fox, lambda_init=0.0, extras=(), kaug=None):
    rr = bsz * seq
    r = qt.shape[2]
    tq = ATT_TILE
    nq = seq // tq
    qmap = np.concatenate([np.full(i + 1, i) for i in range(nq)]).astype(np.int32)
    kmap = np.concatenate([np.arange(i + 1) for i in range(nq)]).astype(np.int32)
    nsteps = int(qmap.shape[0])
    meta_blk = rr // N_META
    qmt = _meta_columns(qt, bsz, rr)
    vmt = _meta_columns(vt, bsz, rr)
    keys = kaug if fox else k
    nkh = keys.shape[0]

    tile_qt = pl.BlockSpec((N_PAIRS, LANES, tq), lambda b, t, qm, km: (0, 0, b * nq + qm[t]))
    tile_vt = pl.BlockSpec((N_PAIRS, LANES, tq), lambda b, t, qm, km: (0, 0, b * nq + km[t]))
    tile_k = pl.BlockSpec((nkh, tq, LANES), lambda b, t, qm, km: (0, b * nq + km[t], 0))
    meta_k = pl.BlockSpec((N_PAIRS, N_META, LANES), lambda b, t, qm, km: (0, meta_blk + b, 0))
    meta_vt = pl.BlockSpec((1, N_PAIRS, LANES, N_META), lambda b, t, qm, km: (b, 0, 0, 0))

    if fox:
        cmeta = extras[0]
        extra_specs = [pl.BlockSpec((1,) + cmeta.shape[1:], lambda b, t, qm, km: (b, 0, 0, 0))]
    else:
        extra_specs = [pl.BlockSpec(a.shape, functools.partial(lambda nd, b, t, qm, km: (0,) * nd,
                                                               a.ndim)) for a in extras]

    ot = pl.pallas_call(
        functools.partial(_attn_kernel, fox=fox, lambda_init=lambda_init),
        grid_spec=pltpu.PrefetchScalarGridSpec(
            num_scalar_prefetch=2, grid=(bsz, nsteps),
            in_specs=[tile_qt, tile_k, tile_vt, meta_k, meta_vt] + extra_specs,
            out_specs=tile_qt,
            scratch_shapes=[pltpu.VMEM((N_PAIRS, 1, tq), F32)] * 2
            + [pltpu.VMEM((N_PAIRS, LANES if fox else LANES + ONES_ROWS, tq), F32)] * 2),
        out_shape=jax.ShapeDtypeStruct((N_PAIRS, LANES, rr), BF16),
        compiler_params=pltpu.CompilerParams(
            dimension_semantics=("arbitrary", "arbitrary"), vmem_limit_bytes=VMEM_LIMIT),
        name="attn_fox" if fox else "attn_diff",
    )(jnp.asarray(qmap), jnp.asarray(kmap), qt, keys, vt, k, vmt, *extras)

    per_b = lambda b: (b, 0, 0, 0)
    if fox:
        m_extras = (extras[0],)
        m_specs = [pl.BlockSpec((1,) + extras[0].shape[1:], per_b)]
    else:
        m_extras = (extras[0], extras[2], extras[3])
        m_specs = [pl.BlockSpec(a.shape, functools.partial(lambda nd, b: (0,) * nd, a.ndim))
                   for a in m_extras]
    ot_meta = pl.pallas_call(
        functools.partial(_attn_meta_kernel, fox=fox, lambda_init=lambda_init),
        grid=(bsz,),
        in_specs=[pl.BlockSpec((1, N_PAIRS, LANES, N_META), per_b),
                  pl.BlockSpec((N_PAIRS, N_META, LANES), lambda b: (0, meta_blk + b, 0)),
                  pl.BlockSpec((1, N_PAIRS, LANES, N_META), per_b)] + m_specs,
        out_specs=pl.BlockSpec((1, N_PAIRS, LANES, N_META), per_b),
        out_shape=jax.ShapeDtypeStruct((bsz, N_PAIRS, LANES, N_META), BF16),
        compiler_params=pltpu.CompilerParams(dimension_semantics=("arbitrary",)),
        name="attn_fox_meta" if fox else "attn_diff_meta",
    )(qmt, k, vmt, *m_extras)
    tail = ot_meta.transpose(1, 2, 0, 3).reshape(N_PAIRS, LANES, bsz * N_META)
    tail = jnp.pad(tail, ((0, 0), (0, 0), (0, r - rr - bsz * N_META)))
    return ot, tail


def _layer_norm(y, g, b):
    mu = jnp.mean(y, axis=1, keepdims=True)
    yc = y - mu
    var = jnp.mean(yc * yc, axis=1, keepdims=True)
    return yc * lax.rsqrt(var + LN_EPS) * g + b


def _mix_router_kernel(ot_ref, otm_ref, h_ref, wo_ref, g_ref, b_ref, wr_ref, br_ref,
                       h1_ref, route_ref, counts_ref, carry_ref):
    i = pl.program_id(0)
    tm = h_ref.shape[0]

    @pl.when(i == 0)
    def _():
        carry_ref[...] = jnp.zeros_like(carry_ref)

    is_tail = i == pl.num_programs(0) - 1
    ot = jnp.concatenate([jnp.where(is_tail, otm_ref[j], ot_ref[j]) for j in range(N_PAIRS)],
                         axis=0)
    mix = lax.dot_general(ot, wo_ref[...], _TN, preferred_element_type=F32)
    h1 = _layer_norm(DN_ALPHA * h_ref[...] + mix, g_ref[...], b_ref[...])
    h1_ref[...] = h1

    logits = jnp.dot(h1.astype(BF16), wr_ref[...], preferred_element_type=F32) + br_ref[...]
    lane = lax.broadcasted_iota(jnp.int32, logits.shape, 1)
    lane_f = lane.astype(F32)
    big = float(LANES)
    g_mask = lane < N_GROUPS
    lg = jnp.where(g_mask, logits, NEG_INF)
    gmax = jnp.max(lg, axis=1, keepdims=True)
    gsel = jnp.min(jnp.where(lg == gmax, lane_f, big), axis=1, keepdims=True)
    p_group = 1.0 / jnp.sum(jnp.where(g_mask, jnp.exp(lg - gmax), 0.0), axis=1, keepdims=True)
    e_lo = N_GROUPS + EXPERTS_PER_GROUP * gsel
    e_mask = jnp.logical_and(lane_f >= e_lo, lane_f < e_lo + EXPERTS_PER_GROUP)
    le = jnp.where(e_mask, logits, NEG_INF)
    v1 = jnp.max(le, axis=1, keepdims=True)
    i1 = jnp.min(jnp.where(jnp.logical_and(le == v1, e_mask), lane_f, big),
                 axis=1, keepdims=True)
    e_mask2 = jnp.logical_and(e_mask, lane_f != i1)
    le2 = jnp.where(e_mask2, logits, NEG_INF)
    v2 = jnp.max(le2, axis=1, keepdims=True)
    i2 = jnp.min(jnp.where(jnp.logical_and(le2 == v2, e_mask2), lane_f, big),
                 axis=1, keepdims=True)
    ex = jnp.exp(v2 - v1)
    gate0 = p_group * (1.0 / (1.0 + ex))
    gate1 = p_group * (ex / (1.0 + ex))
    e0 = i1 - N_GROUPS
    e1 = i2 - N_GROUPS

    onehot = jnp.logical_or(lane_f == e0, lane_f == e1)
    oh_bf = jnp.where(onehot, 1.0, 0.0).astype(BF16)
    ltri = jnp.where(lax.broadcasted_iota(jnp.int32, (tm, tm), 1)
                     < lax.broadcasted_iota(jnp.int32, (tm, tm), 0), 1.0, 0.0).astype(BF16)
    before = jnp.dot(ltri, oh_bf, preferred_element_type=F32) + carry_ref[...]
    r0 = jnp.sum(jnp.where(lane_f == e0, before, 0.0), axis=1, keepdims=True)
    r1 = jnp.sum(jnp.where(lane_f == e1, before, 0.0), axis=1, keepdims=True)
    carry = carry_ref[...] + jnp.sum(jnp.where(onehot, 1.0, 0.0), axis=0, keepdims=True)
    carry_ref[...] = carry
    counts_ref[...] = carry

    route = jnp.zeros_like(logits)
    for col, val in enumerate((e0, e1, r0, r1, gate0, gate1)):
        route = jnp.where(lane == col, val, route)
    route_ref[...] = route


def _mix_and_route(ot, ot_tail, h, wo_bf, ln_g, ln_b, wr_bf, br):
    r, d = h.shape
    tm = ROW_TILE
    assert ot_tail.shape[2] == tm and ot.shape[2] == r - tm
    last_real = ot.shape[2] // tm - 1
    row = lambda i: (i, 0)
    const = lambda i: (0, 0)
    return pl.pallas_call(
        _mix_router_kernel, grid=(r // tm,),
        in_specs=[pl.BlockSpec((N_PAIRS, LANES, tm), lambda i: (0, 0, jnp.minimum(i, last_real))),
                  pl.BlockSpec((N_PAIRS, LANES, tm), lambda i: (0, 0, 0)),
                  pl.BlockSpec((tm, d), row),
                  pl.BlockSpec((d, d), const),
                  pl.BlockSpec((1, d), const), pl.BlockSpec((1, d), const),
                  pl.BlockSpec((d, LANES), const), pl.BlockSpec((1, LANES), const)],
        out_specs=[pl.BlockSpec((tm, d), row), pl.BlockSpec((tm, LANES), row),
                   pl.BlockSpec((1, LANES), const)],
        out_shape=[jax.ShapeDtypeStruct((r, d), F32), jax.ShapeDtypeStruct((r, LANES), F32),
                   jax.ShapeDtypeStruct((1, LANES), F32)],
        scratch_shapes=[pltpu.VMEM((1, LANES), F32)],
        compiler_params=pltpu.CompilerParams(dimension_semantics=("arbitrary",),
                                             vmem_limit_bytes=VMEM_LIMIT),
        name="mix_router")(ot, ot_tail, h, wo_bf, ln_g, ln_b, wr_bf, br)


def _row_copy(src_ref, src_row, dst_ref, dst_row, sem):
    return pltpu.make_async_copy(src_ref.at[pl.ds(src_row, 1), :],
                                 dst_ref.at[pl.ds(dst_row, 1), :], sem)


def _fetch_slots(slots_ref, slot_smem, idx_sem):
    i = pl.program_id(0)
    n = slot_smem.shape[0] // 2
    cur = lax.rem(i, 2)

    def copy(step, buf):
        dst = slot_smem.at[pl.ds(pl.multiple_of(buf * n, n), n)]
        return pltpu.make_async_copy(slots_ref.at[step], dst, idx_sem.at[buf])

    @pl.when(i == 0)
    def _():
        copy(0, 0).start()

    @pl.when(i + 1 < pl.num_programs(0))
    def _():
        copy(i + 1, 1 - cur).start()

    copy(i, cur).wait()
    return cur * n


def _dispatch_kernel(slots_ref, h_ref, xs_in_ref, xs_ref, slot_smem, sem, idx_sem):
    del xs_in_ref
    tm = h_ref.shape[0]
    base = _fetch_slots(slots_ref, slot_smem, idx_sem)

    def start(t, carry):
        for c in range(2):
            _row_copy(h_ref, t, xs_ref, slot_smem[base + c * tm + t], sem).start()
        return carry

    def wait(t, carry):
        for c in range(2):
            _row_copy(h_ref, t, xs_ref, slot_smem[base + c * tm + t], sem).wait()
        return carry

    lax.fori_loop(0, tm, start, 0, unroll=DMA_UNROLL)
    lax.fori_loop(0, tm, wait, 0, unroll=DMA_UNROLL)


def _dispatch(h1, slots, n_rows):
    r, d = h1.shape
    tm = ROW_TILE
    xs0 = jnp.zeros((n_rows, d), F32)
    any_spec = pl.BlockSpec(memory_space=pl.ANY)
    return pl.pallas_call(
        _dispatch_kernel, grid=(r // tm,),
        in_specs=[any_spec, pl.BlockSpec((tm, d), lambda i: (i, 0)), any_spec],
        out_specs=any_spec,
        out_shape=jax.ShapeDtypeStruct((n_rows, d), F32),
        scratch_shapes=[pltpu.SMEM((4 * tm,), jnp.int32), pltpu.SemaphoreType.DMA,
                        pltpu.SemaphoreType.DMA((2,))],
        input_output_aliases={2: 0},
        compiler_params=pltpu.CompilerParams(dimension_semantics=("arbitrary",)),
        name="moe_dispatch")(slots, h1, xs0)


def _expert_kernel(tile_e_ref, n_used_ref, x_ref, wg_ref, wu_ref, wd_ref, y_ref):
    del tile_e_ref

    @pl.when(pl.program_id(0) < n_used_ref[0])
    def _():
        xb = x_ref[...].astype(BF16)
        g = jnp.dot(xb, wg_ref[0].astype(BF16), preferred_element_type=F32)
        u = jnp.dot(xb, wu_ref[0].astype(BF16), preferred_element_type=F32)
        a = (g * jax.nn.sigmoid(g) * u).astype(BF16)
        y_ref[...] = jnp.dot(a, wd_ref[0].astype(BF16), preferred_element_type=F32)

    @pl.when(pl.program_id(0) >= n_used_ref[0])
    def _():
        y_ref[...] = jnp.zeros_like(y_ref)


def _experts(xs, tile_e, n_used, w_gate, w_up, w_down):
    p, d = xs.shape
    te = EXP_TILE
    de = w_gate.shape[2]
    row = lambda i, te_ref, nu_ref: (jnp.minimum(i, nu_ref[0] - 1), 0)
    wsel = lambda i, te_ref, nu_ref: (te_ref[i], 0, 0)
    return pl.pallas_call(
        _expert_kernel,
        grid_spec=pltpu.PrefetchScalarGridSpec(
            num_scalar_prefetch=2, grid=(p // te,),
            in_specs=[pl.BlockSpec((te, d), row),
                      pl.BlockSpec((1, d, de), wsel), pl.BlockSpec((1, d, de), wsel),
                      pl.BlockSpec((1, de, d), wsel)],
            out_specs=pl.BlockSpec((te, d), lambda i, te_ref, nu_ref: (i, 0))),
        out_shape=jax.ShapeDtypeStruct((p, d), F32),
        compiler_params=pltpu.CompilerParams(dimension_semantics=("arbitrary",),
                                             vmem_limit_bytes=VMEM_LIMIT),
        name="moe_experts")(tile_e, n_used, xs, w_gate, w_up, w_down)


def _combine_kernel(slots_ref, ys_ref, route_ref, h_ref, g_ref, b_ref, out_ref,
                    slot_smem, y0_ref, y1_ref, sem, idx_sem):
    tm = h_ref.shape[0]
    idx_copy = pltpu.make_async_copy(slots_ref.at[pl.program_id(0)], slot_smem, idx_sem)
    idx_copy.start()
    idx_copy.wait()
    base = 0
    bufs = (y0_ref, y1_ref)

    def start(t, carry):
        for c in range(2):
            _row_copy(ys_ref, slot_smem[base + c * tm + t], bufs[c], t, sem).start()
        return carry

    def wait(t, carry):
        for c in range(2):
            _row_copy(ys_ref, slot_smem[base + c * tm + t], bufs[c], t, sem).wait()
        return carry

    lax.fori_loop(0, tm, start, 0, unroll=DMA_UNROLL)
    lax.fori_loop(0, tm, wait, 0, unroll=DMA_UNROLL)
    route = route_ref[...]
    ffn = route[:, 4:5] * y0_ref[...] + route[:, 5:6] * y1_ref[...]
    out_ref[...] = _layer_norm(DN_ALPHA * h_ref[...] + ffn, g_ref[...], b_ref[...])


def _combine(ys, slots, route, h1, ln_g, ln_b, n_out_rows):
    r, d = h1.shape
    tm = ROW_TILE
    row = lambda i: (i, 0)
    const = lambda i: (0, 0)
    any_spec = pl.BlockSpec(memory_space=pl.ANY)
    return pl.pallas_call(
        _combine_kernel, grid=(n_out_rows // tm,),
        in_specs=[any_spec, any_spec, pl.BlockSpec((tm, LANES), row),
                  pl.BlockSpec((tm, d), row), pl.BlockSpec((1, d), const),
                  pl.BlockSpec((1, d), const)],
        out_specs=pl.BlockSpec((tm, d), row),
        out_shape=jax.ShapeDtypeStruct((n_out_rows, d), F32),
        scratch_shapes=[pltpu.SMEM((2 * tm,), jnp.int32), pltpu.VMEM((tm, d), F32),
                        pltpu.VMEM((tm, d), F32), pltpu.SemaphoreType.DMA,
                        pltpu.SemaphoreType.DMA],
        compiler_params=pltpu.CompilerParams(dimension_semantics=("arbitrary",),
                                             vmem_limit_bytes=VMEM_LIMIT),
        name="moe_combine")(slots, ys, route, h1, ln_g, ln_b)


def _moe(h1, route, counts, w_gate, w_up, w_down, ln_g, ln_b, n_out_rows):
    r, _ = h1.shape
    tm, te = ROW_TILE, EXP_TILE
    n_tiles = (2 * r) // te + N_EXPERTS
    cnt = counts[0, :N_EXPERTS].astype(jnp.int32)
    seg_tiles = (cnt + te - 1) // te
    seg_end = jnp.cumsum(seg_tiles)
    seg_start = (seg_end - seg_tiles) * te
    n_used = seg_end[-1:]
    tile_ids = jnp.arange(n_tiles, dtype=jnp.int32)
    tile_e = jnp.sum(tile_ids[:, None] >= seg_end[None, :], axis=1)
    last_e = jnp.sum(n_used - 1 >= seg_end)
    tile_e = jnp.minimum(tile_e, last_e).astype(jnp.int32)
    e01 = route[:, 0:2].astype(jnp.int32)
    slot = seg_start[e01] + route[:, 2:4].astype(jnp.int32)
    slots = slot.reshape(r // tm, tm, 2).transpose(0, 2, 1).reshape(r // tm, 2 * tm)

    xs = _dispatch(h1, slots, n_tiles * te)
    ys = _experts(xs, tile_e, n_used.astype(jnp.int32), w_gate, w_up, w_down)
    return _combine(ys, slots, route, h1, ln_g, ln_b, n_out_rows)


def _diff_tables(rel_bias):
    bkt = _bucket_table(2 * LANES + N_META)
    rel = (rel_bias - rel_bias[N_BUCKETS - 1:N_BUCKETS]).T * LOG2E
    key = np.arange(LANES)[:, None]
    qry = np.arange(LANES)[None, :]
    d0 = jnp.where(jnp.asarray(qry >= key), rel[:, bkt[np.maximum(qry - key, 0)]], NEG_INF)
    d1 = rel[:, bkt[LANES + qry - key]]
    dtab = jnp.stack([d0, d1], axis=1)
    km = np.arange(N_META)[:, None]
    mtab = rel[:, bkt[N_META + qry - km]]
    return dtab.astype(F32), mtab.astype(F32)


def kernel(x, meta_tokens, rel_bias, diff_w_qkv, diff_lambda, diff_subln, diff_w_o,
           fox_w_in, fox_b_f, fox_w_o, ln_mix_g, ln_mix_b, ln_ffn_g, ln_ffn_b,
           router_group_w, router_group_b, router_expert_w, router_expert_b,
           expert_w_gate, expert_w_up, expert_w_down):
    bsz, seq, d = x.shape
    rr = bsz * seq
    r = rr + ROW_TILE
    assert seq % ATT_TILE == 0 and bsz * N_META <= ROW_TILE and d == N_PAIRS * LANES
    n_pad = r - rr - bsz * N_META
    h = jnp.concatenate([x.reshape(rr, d),
                         jnp.tile(meta_tokens.astype(x.dtype), (bsz, 1)),
                         jnp.zeros((n_pad, d), x.dtype)], axis=0)

    for i in range(DEPTH):
        j = i // 2
        if i % 2 == 0:
            lam0 = _lambda_init(i)
            qt, k, vt = _project(h, diff_w_qkv[j])
            dtab, mtab = _diff_tables(rel_bias)
            subln = jnp.broadcast_to(diff_subln[j].astype(F32)[:, None], (2 * HEAD_DIM, ATT_TILE))
            ot, ot_tail = _attention(qt, k, vt, bsz, seq, fox=False, lambda_init=lam0,
                                     extras=(dtab, mtab, diff_lambda[j].astype(F32), subln))
            w_o = diff_w_o[j]
        else:
            w_in = fox_w_in[j]
            qt, k, vt, lft = _project(h, w_in[:, :3 * d], w_in[:, 3 * d:].T.astype(BF16),
                                      fox_b_f[j].astype(F32)[:, None])
            cmeta, kaug = _fox_cumsum(lft, k, bsz, seq)
            ot, ot_tail = _attention(qt, k, vt, bsz, seq, fox=True,
                                     extras=(cmeta[..., None],), kaug=kaug)
            w_o = fox_w_o[j]

        n_router = N_GROUPS + N_EXPERTS
        wr = jnp.concatenate([router_group_w[i], router_expert_w[i].reshape(d, N_EXPERTS)], axis=1)
        wr = jnp.pad(wr, ((0, 0), (0, LANES - n_router))).astype(BF16)
        br = jnp.concatenate([router_group_b[i], router_expert_b[i].reshape(N_EXPERTS)])
        br = jnp.pad(br, (0, LANES - n_router)).astype(F32)[None, :]
        h1, route, counts = _mix_and_route(ot, ot_tail, h, w_o.astype(BF16), ln_mix_g[i][None, :],
                                           ln_mix_b[i][None, :], wr, br)
        n_out = rr if i == DEPTH - 1 else r
        h = _moe(h1, route, counts, expert_w_gate[i], expert_w_up[i], expert_w_down[i],
                 ln_ffn_g[i][None, :], ln_ffn_b[i][None, :], n_out)

    return h.reshape(bsz, seq, d)
```

```python
import functools
import math

import numpy as np
import jax
import jax.numpy as jnp
from jax import lax
from jax.experimental import pallas as pl
from jax.experimental.pallas import tpu as pltpu

F32 = jnp.float32
BF16 = jnp.bfloat16

N_META = 16
HEAD_DIM = 64
N_PAIRS = 8
LANES = 128
N_BUCKETS = 32
MAX_EXACT = 16
MAX_DISTANCE = 128
N_GROUPS = 4
EXPERTS_PER_GROUP = 8
N_EXPERTS = 32
DEPTH = 2
DN_ALPHA = (2 * DEPTH) ** 0.25
LN_EPS = 1e-5
NEG_INF = -1e30
CK_PIECES = 3
LOG2E = math.log2(math.e)
ONES_ROWS = 16

ROW_TILE = 512
ATT_TILE = 512
PAIR_GROUP = 2
EXP_TILE = 512
DMA_UNROLL = 8
VMEM_LIMIT = 48 * 1024 * 1024

_NT = (((1,), (1,)), ((), ()))
_TN = (((0,), (0,)), ((), ()))


def _lambda_init(layer_idx):
    return 0.8 - 0.6 * math.exp(-0.3 * layer_idx)


def _bucket_table(n):
    d = np.arange(n)
    nf = np.maximum(d, 1).astype(np.float64)
    large = MAX_EXACT + (np.log(nf / MAX_EXACT) / math.log(MAX_DISTANCE / MAX_EXACT)
                         * (N_BUCKETS - MAX_EXACT)).astype(np.int64)
    large = np.minimum(large, N_BUCKETS - 1)
    return np.where(d < MAX_EXACT, d, large).astype(np.int32)


def _proj_body(x_ref, wqt_ref, wk_ref, wvt_ref, qt_ref, k_ref, vt_ref):
    xb = x_ref[...].astype(BF16)
    qt = lax.dot_general(wqt_ref[...], xb, _NT, preferred_element_type=F32) * (HEAD_DIM ** -0.5 * LOG2E)
    kk = jnp.dot(xb, wk_ref[...], preferred_element_type=F32)
    vt = lax.dot_general(wvt_ref[...], xb, _NT, preferred_element_type=F32)
    for j in range(N_PAIRS):
        sl = slice(j * LANES, (j + 1) * LANES)
        qt_ref[j] = qt[sl, :].astype(BF16)
        k_ref[j] = kk[:, sl].astype(BF16)
        vt_ref[j] = vt[sl, :].astype(BF16)
    return xb


def _proj_kernel(x_ref, wqt_ref, wk_ref, wvt_ref, qt_ref, k_ref, vt_ref):
    _proj_body(x_ref, wqt_ref, wk_ref, wvt_ref, qt_ref, k_ref, vt_ref)


def _proj_fox_kernel(x_ref, wqt_ref, wk_ref, wvt_ref, wf_ref, bf_ref,
                     qt_ref, k_ref, vt_ref, lft_ref):
    xb = _proj_body(x_ref, wqt_ref, wk_ref, wvt_ref, qt_ref, k_ref, vt_ref)
    z = lax.dot_general(wf_ref[...], xb, _NT, preferred_element_type=F32) + bf_ref[...]
    lft_ref[...] = jnp.minimum(z, 0.0) - jnp.log1p(jnp.exp(-jnp.abs(z)))


def _project(h, w_qkv, wf_t=None, bf_col=None):
    r, d = h.shape
    tm = ROW_TILE
    wqt = w_qkv[:, :d].T.astype(BF16)
    wk = w_qkv[:, d:2 * d].astype(BF16)
    wvt = w_qkv[:, 2 * d:3 * d].T.astype(BF16)
    t_shape = jax.ShapeDtypeStruct((N_PAIRS, LANES, r), BF16)
    k_shape = jax.ShapeDtypeStruct((N_PAIRS, r, LANES), BF16)
    t_spec = pl.BlockSpec((N_PAIRS, LANES, tm), lambda i: (0, 0, i))
    k_spec = pl.BlockSpec((N_PAIRS, tm, LANES), lambda i: (0, i, 0))
    w_spec = pl.BlockSpec((d, d), lambda i: (0, 0))
    in_specs = [pl.BlockSpec((tm, d), lambda i: (i, 0)), w_spec, w_spec, w_spec]
    params = pltpu.CompilerParams(dimension_semantics=("arbitrary",),
                                  vmem_limit_bytes=VMEM_LIMIT)
    if wf_t is None:
        return pl.pallas_call(
            _proj_kernel, grid=(r // tm,), in_specs=in_specs,
            out_specs=[t_spec, k_spec, t_spec], out_shape=[t_shape, k_shape, t_shape],
            compiler_params=params, name="proj_diff")(h, wqt, wk, wvt)
    nh = wf_t.shape[0]
    in_specs += [pl.BlockSpec((nh, d), lambda i: (0, 0)),
                 pl.BlockSpec((nh, 1), lambda i: (0, 0))]
    return pl.pallas_call(
        _proj_fox_kernel, grid=(r // tm,), in_specs=in_specs,
        out_specs=[t_spec, k_spec, t_spec, pl.BlockSpec((nh, tm), lambda i: (0, i))],
        out_shape=[t_shape, k_shape, t_shape, jax.ShapeDtypeStruct((nh, r), F32)],
        compiler_params=params, name="proj_fox")(h, wqt, wk, wvt, wf_t, bf_col)


def _ck_lane(c):
    return HEAD_DIM if c == 0 else 0


def _placement_matrices(nh):
    e = np.zeros((nh, CK_PIECES * LANES, LANES), np.float32)
    for h in range(nh):
        for p in range(CK_PIECES):
            e[h, p * LANES + h, _ck_lane(h % 2) + p] = 1.0
    return e


def _cumsum_kernel(lfm_ref, lft_ref, k_ref, place_ref, cmeta_ref, kaug_ref, carry_ref):
    c = pl.program_id(1)
    nh, tc = lft_ref.shape

    @pl.when(c == 0)
    def _():
        lm = lfm_ref[0]
        n = lm.shape[1]
        tri = (lax.broadcasted_iota(jnp.int32, (n, n), 0)
               <= lax.broadcasted_iota(jnp.int32, (n, n), 1)).astype(F32)
        cm = jnp.dot(lm, tri, preferred_element_type=F32, precision=lax.Precision.HIGHEST)
        cmeta_ref[0] = cm * LOG2E
        carry_ref[...] = cm[:, n - 1:n]

    tri = (lax.broadcasted_iota(jnp.int32, (tc, tc), 0)
           <= lax.broadcasted_iota(jnp.int32, (tc, tc), 1)).astype(F32)
    cs = jnp.dot(lft_ref[...], tri, preferred_element_type=F32,
                 precision=lax.Precision.HIGHEST) + carry_ref[...]
    carry_ref[...] = cs[:, tc - 1:tc]

    cst = jnp.concatenate([cs * LOG2E, jnp.zeros((LANES - nh, tc), F32)], axis=0).T
    hi = cst.astype(BF16)
    rem = cst - hi.astype(F32)
    mid = rem.astype(BF16)
    lo = (rem - mid.astype(F32)).astype(BF16)
    pieces = jnp.concatenate([hi, mid, lo], axis=1)
    lane = lax.broadcasted_iota(jnp.int32, (tc, LANES), 1)
    for h in range(nh):
        ck = jnp.dot(pieces, place_ref[h], preferred_element_type=F32).astype(BF16)
        own = (lane < HEAD_DIM) if h % 2 == 0 else (lane >= HEAD_DIM)
        kaug_ref[h] = jnp.where(own, k_ref[h // 2], ck)


def _fox_cumsum(lft, k, bsz, seq):
    nh, _ = lft.shape
    rr = bsz * seq
    tc = ROW_TILE
    nc = seq // tc
    lf_meta = lft[:, rr:rr + bsz * N_META].reshape(nh, bsz, N_META).transpose(1, 0, 2)
    place = jnp.asarray(_placement_matrices(nh), BF16)
    return pl.pallas_call(
        _cumsum_kernel, grid=(bsz, nc),
        in_specs=[pl.BlockSpec((1, nh, N_META), lambda b, c: (b, 0, 0)),
                  pl.BlockSpec((nh, tc), lambda b, c: (0, b * nc + c)),
                  pl.BlockSpec((N_PAIRS, tc, LANES), lambda b, c: (0, b * nc + c, 0)),
                  pl.BlockSpec(place.shape, lambda b, c: (0, 0, 0))],
        out_specs=[pl.BlockSpec((1, nh, N_META), lambda b, c: (b, 0, 0)),
                   pl.BlockSpec((nh, tc, LANES), lambda b, c: (0, b * nc + c, 0))],
        out_shape=[jax.ShapeDtypeStruct((bsz, nh, N_META), F32),
                   jax.ShapeDtypeStruct((nh, rr, LANES), BF16)],
        scratch_shapes=[pltpu.VMEM((nh, 1), F32)],
        compiler_params=pltpu.CompilerParams(dimension_semantics=("arbitrary", "arbitrary"),
                                             vmem_limit_bytes=VMEM_LIMIT),
        name="fox_cumsum")(lf_meta, lft, k, place)


def _head_q(qt, c, fox):
    row = lax.broadcasted_iota(jnp.int32, qt.shape, 0)
    own = (row < HEAD_DIM) if c == 0 else (row >= HEAD_DIM)
    if fox:
        lo = _ck_lane(c)
        fill = jnp.where(jnp.logical_and(row >= lo, row < lo + CK_PIECES), -1.0, 0.0).astype(qt.dtype)
    else:
        fill = jnp.zeros_like(qt)
    return jnp.where(own, qt, fill)


def _head_k(kp, c):
    lane = lax.broadcasted_iota(jnp.int32, kp.shape, 1)
    own = (lane < HEAD_DIM) if c == 0 else (lane >= HEAD_DIM)
    return jnp.where(own, kp, jnp.zeros_like(kp))


def _diag_bias(d0, d1, n):
    zero = jnp.zeros_like(d0)
    neg = jnp.full_like(d0, NEG_INF)
    rows = []
    for r in range(n):
        blocks = []
        for c in range(n):
            if r > c:
                blocks.append(neg)
            elif r == c:
                blocks.append(d0)
            elif r == c - 1 and d1 is not None:
                blocks.append(d1)
            else:
                blocks.append(zero)
        rows.append(jnp.concatenate(blocks, axis=1))
    return jnp.concatenate(rows, axis=0)


def _causal_block():
    key = lax.broadcasted_iota(jnp.int32, (LANES, LANES), 0)
    qry = lax.broadcasted_iota(jnp.int32, (LANES, LANES), 1)
    return jnp.where(key > qry, NEG_INF, 0.0).astype(F32)


def _diff_lambda(lam_ref, lambda_init):
    lam = lam_ref[...]
    a = jnp.sum(lam[0:1] * lam[1:2], axis=1, keepdims=True)
    b = jnp.sum(lam[2:3] * lam[3:4], axis=1, keepdims=True)
    return jnp.exp(a) - jnp.exp(b) + lambda_init


def _head_v(vt, c, fox):
    if fox:
        row = lax.broadcasted_iota(jnp.int32, vt.shape, 0)
        own = (row < HEAD_DIM) if c == 0 else (row >= HEAD_DIM)
        return jnp.where(own, vt, jnp.ones_like(vt))
    return jnp.concatenate([vt, jnp.ones((ONES_ROWS, vt.shape[1]), vt.dtype)], axis=0)


def _denominator(acc, c, fox):
    r = _ck_lane(c) if fox else 2 * HEAD_DIM
    return acc[r:r + 1, :]


def _diff_finish(acc0, acc1, lam_full, subln, lambda_init):
    nv = 2 * HEAD_DIM
    o = (acc0[:nv] / _denominator(acc0, 0, False)
         - lam_full * (acc1[:nv] / _denominator(acc1, 1, False)))
    ms = jnp.mean(o * o, axis=0, keepdims=True)
    return o * lax.rsqrt(ms + LN_EPS) * subln * (1.0 - lambda_init)


def _fox_finish(acc0, acc1):
    row = lax.broadcasted_iota(jnp.int32, acc0.shape, 0)
    return jnp.where(row < HEAD_DIM, acc0 / _denominator(acc0, 0, True),
                     acc1 / _denominator(acc1, 1, True))


def _attn_kernel(qmap_ref, kmap_ref, qt_ref, k_ref, vt_ref, km_ref, vmt_ref, *rest,
                 fox, lambda_init):
    if fox:
        ckm_ref, ot_ref = rest[:2]
    else:
        dtab_ref, mtab_ref, lam_ref, subln_ref, ot_ref = rest[:5]
    m_sc, acc_sc, s_sc = rest[-6:-4], rest[-4:-2], rest[-2:]
    t = pl.program_id(1)
    qi = qmap_ref[t]
    ki = kmap_ref[t]
    tq = qt_ref.shape[2]
    nblk = tq // LANES

    def init_from_meta(j, c, qa, near):
        mi = 2 * j + c
        km = km_ref[j]
        if fox:
            s = jnp.dot(_head_k(km, c), qa, preferred_element_type=F32) - ckm_ref[0, mi]
        else:
            s = jnp.dot(km, qa, preferred_element_type=F32)
            if near:
                s = s + jnp.concatenate(
                    [mtab_ref[mi], jnp.zeros((N_META, tq - LANES), F32)], axis=1)
        m = jnp.max(s, axis=0, keepdims=True)
        p = jnp.exp2(s - m)
        m_sc[c][j] = m
        acc_sc[c][j] = jnp.dot(_head_v(vmt_ref[0, j], c, fox), p.astype(BF16),
                               preferred_element_type=F32)

    def tile_scores(j, c, qa, diag):
        mi = 2 * j + c
        kk = k_ref[mi] if fox else k_ref[j]
        s = jnp.dot(kk, qa, preferred_element_type=F32)
        if diag:
            if fox:
                s = s + _diag_bias(_causal_block(), None, nblk)
            else:
                s = s + _diag_bias(dtab_ref[mi, 0], dtab_ref[mi, 1], nblk)
        elif not fox:
            near = jnp.where(ki == qi - 1, 1.0, 0.0).astype(F32)
            corner = s[tq - LANES:, :LANES] + near * dtab_ref[mi, 1]
            bottom = jnp.concatenate([corner, s[tq - LANES:, LANES:]], axis=1)
            s = jnp.concatenate([s[:tq - LANES], bottom], axis=0)
        return s

    def tile_probs(j, c, s):
        m_prev = m_sc[c][j]
        m_new = jnp.maximum(m_prev, jnp.max(s, axis=0, keepdims=True))
        alpha = jnp.exp2(m_prev - m_new)
        p = jnp.exp2(s - m_new)
        m_sc[c][j] = m_new
        return alpha, p.astype(BF16)

    def tile_values(j, c, alpha, p):
        acc_sc[c][j] = alpha * acc_sc[c][j] + jnp.dot(_head_v(vt_ref[j], c, fox), p,
                                                      preferred_element_type=F32)

    def meta_init(near):
        def pair(j, carry):
            qt = qt_ref[j]
            for c in range(2):
                init_from_meta(j, c, _head_q(qt, c, fox), near)
            return carry
        lax.fori_loop(0, N_PAIRS, pair, 0)

    def run(diag):
        n_groups = N_PAIRS // PAIR_GROUP

        def scores(g):
            for u in range(PAIR_GROUP):
                j = g * PAIR_GROUP + u
                qt = qt_ref[j]
                for c in range(2):
                    s_sc[g % 2][2 * u + c] = tile_scores(j, c, _head_q(qt, c, fox), diag)

        def finish_group(g):
            for u in range(PAIR_GROUP):
                j = g * PAIR_GROUP + u
                ap = [tile_probs(j, c, s_sc[g % 2][2 * u + c]) for c in range(2)]
                for c in range(2):
                    tile_values(j, c, *ap[c])
                if diag:
                    if fox:
                        o = _fox_finish(acc_sc[0][j], acc_sc[1][j])
                    else:
                        o = _diff_finish(acc_sc[0][j], acc_sc[1][j],
                                         _diff_lambda(lam_ref, lambda_init),
                                         subln_ref[...], lambda_init)
                    ot_ref[j] = o.astype(ot_ref.dtype)

        scores(0)
        for g in range(n_groups):
            if g + 1 < n_groups:
                scores(g + 1)
            finish_group(g)

    @pl.when(ki == 0)
    def _():
        if fox:
            meta_init(False)
        else:
            @pl.when(qi == 0)
            def _():
                meta_init(True)

            @pl.when(qi > 0)
            def _():
                meta_init(False)

    @pl.when(ki == qi)
    def _():
        run(True)

    @pl.when(ki < qi)
    def _():
        run(False)


def _attn_meta_kernel(qmt_ref, k_ref, vmt_ref, *rest, fox, lambda_init):
    if fox:
        ckm_ref, ot_ref = rest
    else:
        dtab_ref, lam_ref, subln_ref, ot_ref = rest
    causal = _causal_block()[:N_META, :N_META]

    def pair(j, carry):
        qt = qmt_ref[0, j]
        km = k_ref[j]
        acc = []
        for c in range(2):
            mi = 2 * j + c
            qa = _head_q(qt, c, fox)
            if fox:
                s = (jnp.dot(_head_k(km, c), qa, preferred_element_type=F32)
                     - ckm_ref[0, mi] + causal)
            else:
                s = jnp.dot(km, qa, preferred_element_type=F32) + dtab_ref[mi, 0][:N_META, :N_META]
            m = jnp.max(s, axis=0, keepdims=True)
            p = jnp.exp2(s - m)
            acc.append(jnp.dot(_head_v(vmt_ref[0, j], c, fox), p.astype(BF16),
                               preferred_element_type=F32))
        if fox:
            o = _fox_finish(acc[0], acc[1])
        else:
            o = _diff_finish(acc[0], acc[1], _diff_lambda(lam_ref, lambda_init),
                             subln_ref[...][:, :N_META], lambda_init)
        ot_ref[0, j] = o.astype(ot_ref.dtype)
        return carry

    lax.fori_loop(0, N_PAIRS, pair, 0)


def _meta_columns(xt, bsz, rr):
    cols = xt[:, :, rr:rr + bsz * N_META]
    return cols.reshape(N_PAIRS, LANES, bsz, N_META).transpose(2, 0, 1, 3)


def _attention(qt, k, vt, bsz, seq, *, fox, lambda_init=0.0, extras=(), kaug=None):
    rr = bsz * seq
    r = qt.shape[2]
    tq = ATT_TILE
    nq = seq // tq
    qmap = np.concatenate([np.full(i + 1, i) for i in range(nq)]).astype(np.int32)
    kmap = np.concatenate([np.arange(i + 1) for i in range(nq)]).astype(np.int32)
    nsteps = int(qmap.shape[0])
    meta_blk = rr // N_META
    qmt = _meta_columns(qt, bsz, rr)
    vmt = _meta_columns(vt, bsz, rr)
    keys = kaug if fox else k
    nkh = keys.shape[0]

    tile_qt = pl.BlockSpec((N_PAIRS, LANES, tq), lambda b, t, qm, km: (0, 0, b * nq + qm[t]))
    tile_vt = pl.BlockSpec((N_PAIRS, LANES, tq), lambda b, t, qm, km: (0, 0, b * nq + km[t]))
    tile_k = pl.BlockSpec((nkh, tq, LANES), lambda b, t, qm, km: (0, b * nq + km[t], 0))
    meta_k = pl.BlockSpec((N_PAIRS, N_META, LANES), lambda b, t, qm, km: (0, meta_blk + b, 0))
    meta_vt = pl.BlockSpec((1, N_PAIRS, LANES, N_META), lambda b, t, qm, km: (b, 0, 0, 0))

    if fox:
        cmeta = extras[0]
        extra_specs = [pl.BlockSpec((1,) + cmeta.shape[1:], lambda b, t, qm, km: (b, 0, 0, 0))]
    else:
        extra_specs = [pl.BlockSpec(a.shape, functools.partial(lambda nd, b, t, qm, km: (0,) * nd,
                                                               a.ndim)) for a in extras]

    ot = pl.pallas_call(
        functools.partial(_attn_kernel, fox=fox, lambda_init=lambda_init),
        grid_spec=pltpu.PrefetchScalarGridSpec(
            num_scalar_prefetch=2, grid=(bsz, nsteps),
            in_specs=[tile_qt, tile_k, tile_vt, meta_k, meta_vt] + extra_specs,
            out_specs=tile_qt,
            scratch_shapes=[pltpu.VMEM((N_PAIRS, 1, tq), F32)] * 2
            + [pltpu.VMEM((N_PAIRS, LANES if fox else LANES + ONES_ROWS, tq), F32)] * 2
            + [pltpu.VMEM((2 * PAIR_GROUP, tq, tq), F32)] * 2),
        out_shape=jax.ShapeDtypeStruct((N_PAIRS, LANES, rr), BF16),
        compiler_params=pltpu.CompilerParams(
            dimension_semantics=("arbitrary", "arbitrary"), vmem_limit_bytes=VMEM_LIMIT),
        name="attn_fox" if fox else "attn_diff",
    )(jnp.asarray(qmap), jnp.asarray(kmap), qt, keys, vt, k, vmt, *extras)

    per_b = lambda b: (b, 0, 0, 0)
    if fox:
        m_extras = (extras[0],)
        m_specs = [pl.BlockSpec((1,) + extras[0].shape[1:], per_b)]
    else:
        m_extras = (extras[0], extras[2], extras[3])
        m_specs = [pl.BlockSpec(a.shape, functools.partial(lambda nd, b: (0,) * nd, a.ndim))
                   for a in m_extras]
    ot_meta = pl.pallas_call(
        functools.partial(_attn_meta_kernel, fox=fox, lambda_init=lambda_init),
        grid=(bsz,),
        in_specs=[pl.BlockSpec((1, N_PAIRS, LANES, N_META), per_b),
                  pl.BlockSpec((N_PAIRS, N_META, LANES), lambda b: (0, meta_blk + b, 0)),
                  pl.BlockSpec((1, N_PAIRS, LANES, N_META), per_b)] + m_specs,
        out_specs=pl.BlockSpec((1, N_PAIRS, LANES, N_META), per_b),
        out_shape=jax.ShapeDtypeStruct((bsz, N_PAIRS, LANES, N_META), BF16),
        compiler_params=pltpu.CompilerParams(dimension_semantics=("arbitrary",)),
        name="attn_fox_meta" if fox else "attn_diff_meta",
    )(qmt, k, vmt, *m_extras)
    tail = ot_meta.transpose(1, 2, 0, 3).reshape(N_PAIRS, LANES, bsz * N_META)
    tail = jnp.pad(tail, ((0, 0), (0, 0), (0, r - rr - bsz * N_META)))
    return ot, tail


def _layer_norm(y, g, b):
    mu = jnp.mean(y, axis=1, keepdims=True)
    yc = y - mu
    var = jnp.mean(yc * yc, axis=1, keepdims=True)
    return yc * lax.rsqrt(var + LN_EPS) * g + b


def _mix_router_kernel(ot_ref, otm_ref, h_ref, wo_ref, g_ref, b_ref, wr_ref, br_ref,
                       h1_ref, route_ref, route_t_ref, counts_ref, carry_ref):
    i = pl.program_id(0)
    tm = h_ref.shape[0]

    @pl.when(i == 0)
    def _():
        carry_ref[...] = jnp.zeros_like(carry_ref)

    is_tail = i == pl.num_programs(0) - 1
    ot = jnp.concatenate([jnp.where(is_tail, otm_ref[j], ot_ref[j]) for j in range(N_PAIRS)],
                         axis=0)
    mix = lax.dot_general(ot, wo_ref[...], _TN, preferred_element_type=F32)
    h1 = _layer_norm(DN_ALPHA * h_ref[...] + mix, g_ref[...], b_ref[...])
    h1_ref[...] = h1

    logits = jnp.dot(h1.astype(BF16), wr_ref[...], preferred_element_type=F32) + br_ref[...]
    lane = lax.broadcasted_iota(jnp.int32, logits.shape, 1)
    lane_f = lane.astype(F32)
    big = float(LANES)
    g_mask = lane < N_GROUPS
    lg = jnp.where(g_mask, logits, NEG_INF)
    gmax = jnp.max(lg, axis=1, keepdims=True)
    gsel = jnp.min(jnp.where(lg == gmax, lane_f, big), axis=1, keepdims=True)
    p_group = 1.0 / jnp.sum(jnp.where(g_mask, jnp.exp(lg - gmax), 0.0), axis=1, keepdims=True)
    e_lo = N_GROUPS + EXPERTS_PER_GROUP * gsel
    e_mask = jnp.logical_and(lane_f >= e_lo, lane_f < e_lo + EXPERTS_PER_GROUP)
    le = jnp.where(e_mask, logits, NEG_INF)
    v1 = jnp.max(le, axis=1, keepdims=True)
    i1 = jnp.min(jnp.where(jnp.logical_and(le == v1, e_mask), lane_f, big),
                 axis=1, keepdims=True)
    e_mask2 = jnp.logical_and(e_mask, lane_f != i1)
    le2 = jnp.where(e_mask2, logits, NEG_INF)
    v2 = jnp.max(le2, axis=1, keepdims=True)
    i2 = jnp.min(jnp.where(jnp.logical_and(le2 == v2, e_mask2), lane_f, big),
                 axis=1, keepdims=True)
    ex = jnp.exp(v2 - v1)
    gate0 = p_group * (1.0 / (1.0 + ex))
    gate1 = p_group * (ex / (1.0 + ex))
    e0 = i1 - N_GROUPS
    e1 = i2 - N_GROUPS

    onehot = jnp.logical_or(lane_f == e0, lane_f == e1)
    oh_bf = jnp.where(onehot, 1.0, 0.0).astype(BF16)
    ltri = jnp.where(lax.broadcasted_iota(jnp.int32, (tm, tm), 1)
                     < lax.broadcasted_iota(jnp.int32, (tm, tm), 0), 1.0, 0.0).astype(BF16)
    before = jnp.dot(ltri, oh_bf, preferred_element_type=F32) + carry_ref[...]
    r0 = jnp.sum(jnp.where(lane_f == e0, before, 0.0), axis=1, keepdims=True)
    r1 = jnp.sum(jnp.where(lane_f == e1, before, 0.0), axis=1, keepdims=True)
    carry = carry_ref[...] + jnp.sum(jnp.where(onehot, 1.0, 0.0), axis=0, keepdims=True)
    carry_ref[...] = carry
    counts_ref[...] = carry

    route = jnp.zeros_like(logits)
    for col, val in enumerate((e0, e1, r0, r1, gate0, gate1)):
        route = jnp.where(lane == col, val, route)
    route_ref[...] = route
    route_t_ref[...] = route.T[:route_t_ref.shape[0], :]


def _mix_and_route(ot, ot_tail, h, wo_bf, ln_g, ln_b, wr_bf, br):
    r, d = h.shape
    tm = ROW_TILE
    assert ot_tail.shape[2] == tm and ot.shape[2] == r - tm
    last_real = ot.shape[2] // tm - 1
    row = lambda i: (i, 0)
    const = lambda i: (0, 0)
    return pl.pallas_call(
        _mix_router_kernel, grid=(r // tm,),
        in_specs=[pl.BlockSpec((N_PAIRS, LANES, tm), lambda i: (0, 0, jnp.minimum(i, last_real))),
                  pl.BlockSpec((N_PAIRS, LANES, tm), lambda i: (0, 0, 0)),
                  pl.BlockSpec((tm, d), row),
                  pl.BlockSpec((d, d), const),
                  pl.BlockSpec((1, d), const), pl.BlockSpec((1, d), const),
                  pl.BlockSpec((d, LANES), const), pl.BlockSpec((1, LANES), const)],
        out_specs=[pl.BlockSpec((tm, d), row), pl.BlockSpec((tm, LANES), row),
                   pl.BlockSpec((8, tm), lambda i: (0, i)), pl.BlockSpec((1, LANES), const)],
        out_shape=[jax.ShapeDtypeStruct((r, d), F32), jax.ShapeDtypeStruct((r, LANES), F32),
                   jax.ShapeDtypeStruct((8, r), F32), jax.ShapeDtypeStruct((1, LANES), F32)],
        scratch_shapes=[pltpu.VMEM((1, LANES), F32)],
        compiler_params=pltpu.CompilerParams(dimension_semantics=("arbitrary",),
                                             vmem_limit_bytes=VMEM_LIMIT),
        name="mix_router")(ot, ot_tail, h, wo_bf, ln_g, ln_b, wr_bf, br)


def _row_copy(src_ref, src_row, dst_ref, dst_row, sem):
    return pltpu.make_async_copy(src_ref.at[pl.ds(src_row, 1), :],
                                 dst_ref.at[pl.ds(dst_row, 1), :], sem)


def _fetch_slots(slots_ref, slot_smem, idx_sem):
    i = pl.program_id(0)
    n = slot_smem.shape[0] // 2
    cur = lax.rem(i, 2)

    def copy(step, buf):
        dst = slot_smem.at[pl.ds(pl.multiple_of(buf * n, n), n)]
        return pltpu.make_async_copy(slots_ref.at[step], dst, idx_sem.at[buf])

    @pl.when(i == 0)
    def _():
        copy(0, 0).start()

    @pl.when(i + 1 < pl.num_programs(0))
    def _():
        copy(i + 1, 1 - cur).start()

    copy(i, cur).wait()
    return cur * n


def _dispatch_kernel(slots_ref, h_ref, xs_in_ref, xs_ref, slot_smem, sem, idx_sem):
    del xs_in_ref
    tm = h_ref.shape[0]
    base = _fetch_slots(slots_ref, slot_smem, idx_sem)

    def start(t, carry):
        for c in range(2):
            _row_copy(h_ref, t, xs_ref, slot_smem[base + c * tm + t], sem).start()
        return carry

    def wait(t, carry):
        for c in range(2):
            _row_copy(h_ref, t, xs_ref, slot_smem[base + c * tm + t], sem).wait()
        return carry

    lax.fori_loop(0, tm, start, 0, unroll=DMA_UNROLL)
    lax.fori_loop(0, tm, wait, 0, unroll=DMA_UNROLL)


def _dispatch(h1, slots, n_rows):
    r, d = h1.shape
    tm = ROW_TILE
    xs0 = jnp.zeros((n_rows, d), F32)
    any_spec = pl.BlockSpec(memory_space=pl.ANY)
    return pl.pallas_call(
        _dispatch_kernel, grid=(r // tm,),
        in_specs=[any_spec, pl.BlockSpec((tm, d), lambda i: (i, 0)), any_spec],
        out_specs=any_spec,
        out_shape=jax.ShapeDtypeStruct((n_rows, d), F32),
        scratch_shapes=[pltpu.SMEM((4 * tm,), jnp.int32), pltpu.SemaphoreType.DMA,
                        pltpu.SemaphoreType.DMA((2,))],
        input_output_aliases={2: 0},
        compiler_params=pltpu.CompilerParams(dimension_semantics=("arbitrary",)),
        name="moe_dispatch")(slots, h1, xs0)


def _expert_kernel(tile_e_ref, n_used_ref, x_ref, wg_ref, wu_ref, wd_ref, y_ref):
    del tile_e_ref

    @pl.when(pl.program_id(0) < n_used_ref[0])
    def _():
        xb = x_ref[...].astype(BF16)
        g = jnp.dot(xb, wg_ref[0, 0].astype(BF16), preferred_element_type=F32)
        u = jnp.dot(xb, wu_ref[0, 0].astype(BF16), preferred_element_type=F32)
        a = (g * jax.nn.sigmoid(g) * u).astype(BF16)
        y_ref[...] = jnp.dot(a, wd_ref[0, 0].astype(BF16), preferred_element_type=F32)

    @pl.when(pl.program_id(0) >= n_used_ref[0])
    def _():
        y_ref[...] = jnp.zeros_like(y_ref)


def _experts(xs, tile_e, n_used, layer, w_gate, w_up, w_down):
    p, d = xs.shape
    te = EXP_TILE
    de = w_gate.shape[3]
    row = lambda i, te_ref, nu_ref: (jnp.minimum(i, nu_ref[0] - 1), 0)
    wsel = lambda i, te_ref, nu_ref: (layer, te_ref[i], 0, 0)
    return pl.pallas_call(
        _expert_kernel,
        grid_spec=pltpu.PrefetchScalarGridSpec(
            num_scalar_prefetch=2, grid=(p // te,),
            in_specs=[pl.BlockSpec((te, d), row),
                      pl.BlockSpec((1, 1, d, de), wsel), pl.BlockSpec((1, 1, d, de), wsel),
                      pl.BlockSpec((1, 1, de, d), wsel)],
            out_specs=pl.BlockSpec((te, d), lambda i, te_ref, nu_ref: (i, 0))),
        out_shape=jax.ShapeDtypeStruct((p, d), F32),
        compiler_params=pltpu.CompilerParams(dimension_semantics=("arbitrary",),
                                             vmem_limit_bytes=VMEM_LIMIT),
        name="moe_experts")(tile_e, n_used, xs, w_gate, w_up, w_down)


def _combine_kernel(slots_ref, ys_ref, route_ref, h_ref, g_ref, b_ref, out_ref,
                    slot_smem, y0_ref, y1_ref, sem, idx_sem):
    tm = h_ref.shape[0]
    idx_copy = pltpu.make_async_copy(slots_ref.at[pl.program_id(0)], slot_smem, idx_sem)
    idx_copy.start()
    idx_copy.wait()
    base = 0
    bufs = (y0_ref, y1_ref)

    def start(t, carry):
        for c in range(2):
            _row_copy(ys_ref, slot_smem[base + c * tm + t], bufs[c], t, sem).start()
        return carry

    def wait(t, carry):
        for c in range(2):
            _row_copy(ys_ref, slot_smem[base + c * tm + t], bufs[c], t, sem).wait()
        return carry

    lax.fori_loop(0, tm, start, 0, unroll=DMA_UNROLL)
    lax.fori_loop(0, tm, wait, 0, unroll=DMA_UNROLL)
    route = route_ref[...]
    ffn = route[:, 4:5] * y0_ref[...] + route[:, 5:6] * y1_ref[...]
    out_ref[...] = _layer_norm(DN_ALPHA * h_ref[...] + ffn, g_ref[...], b_ref[...])


def _combine(ys, slots, route, h1, ln_g, ln_b, n_out_rows):
    r, d = h1.shape
    tm = ROW_TILE
    row = lambda i: (i, 0)
    const = lambda i: (0, 0)
    any_spec = pl.BlockSpec(memory_space=pl.ANY)
    return pl.pallas_call(
        _combine_kernel, grid=(n_out_rows // tm,),
        in_specs=[any_spec, any_spec, pl.BlockSpec((tm, LANES), row),
                  pl.BlockSpec((tm, d), row), pl.BlockSpec((1, d), const),
                  pl.BlockSpec((1, d), const)],
        out_specs=pl.BlockSpec((tm, d), row),
        out_shape=jax.ShapeDtypeStruct((n_out_rows, d), F32),
        scratch_shapes=[pltpu.SMEM((2 * tm,), jnp.int32), pltpu.VMEM((tm, d), F32),
                        pltpu.VMEM((tm, d), F32), pltpu.SemaphoreType.DMA,
                        pltpu.SemaphoreType.DMA],
        compiler_params=pltpu.CompilerParams(dimension_semantics=("arbitrary",),
                                             vmem_limit_bytes=VMEM_LIMIT),
        name="moe_combine")(slots, ys, route, h1, ln_g, ln_b)


def _moe(h1, route, route_t, counts, layer, w_gate, w_up, w_down, ln_g, ln_b, n_out_rows):
    r, _ = h1.shape
    tm, te = ROW_TILE, EXP_TILE
    n_tiles = (2 * r) // te + N_EXPERTS
    cnt = counts[0, :N_EXPERTS].astype(jnp.int32)
    seg_tiles = (cnt + te - 1) // te
    seg_end = jnp.cumsum(seg_tiles)
    seg_start = (seg_end - seg_tiles) * te
    n_used = seg_end[-1:]
    tile_ids = jnp.arange(n_tiles, dtype=jnp.int32)
    tile_e = jnp.sum(tile_ids[:, None] >= seg_end[None, :], axis=1)
    last_e = jnp.sum(n_used - 1 >= seg_end)
    tile_e = jnp.minimum(tile_e, last_e).astype(jnp.int32)
    e01 = route_t[0:2].astype(jnp.int32)
    slot = seg_start[e01] + route_t[2:4].astype(jnp.int32)
    slots = slot.reshape(2, r // tm, tm).transpose(1, 0, 2).reshape(r // tm, 2 * tm)

    xs = _dispatch(h1, slots, n_tiles * te)
    ys = _experts(xs, tile_e, n_used.astype(jnp.int32), layer, w_gate, w_up, w_down)
    return _combine(ys, slots, route, h1, ln_g, ln_b, n_out_rows)


def _diff_tables(rel_bias):
    bkt = _bucket_table(2 * LANES + N_META)
    rel = (rel_bias - rel_bias[N_BUCKETS - 1:N_BUCKETS]).T * LOG2E
    key = np.arange(LANES)[:, None]
    qry = np.arange(LANES)[None, :]
    d0 = jnp.where(jnp.asarray(qry >= key), rel[:, bkt[np.maximum(qry - key, 0)]], NEG_INF)
    d1 = rel[:, bkt[LANES + qry - key]]
    dtab = jnp.stack([d0, d1], axis=1)
    km = np.arange(N_META)[:, None]
    mtab = rel[:, bkt[N_META + qry - km]]
    return dtab.astype(F32), mtab.astype(F32)


def kernel(x, meta_tokens, rel_bias, diff_w_qkv, diff_lambda, diff_subln, diff_w_o,
           fox_w_in, fox_b_f, fox_w_o, ln_mix_g, ln_mix_b, ln_ffn_g, ln_ffn_b,
           router_group_w, router_group_b, router_expert_w, router_expert_b,
           expert_w_gate, expert_w_up, expert_w_down):
    bsz, seq, d = x.shape
    rr = bsz * seq
    r = rr + ROW_TILE
    assert seq % ATT_TILE == 0 and bsz * N_META <= ROW_TILE and d == N_PAIRS * LANES
    n_pad = r - rr - bsz * N_META
    h = jnp.concatenate([x.reshape(rr, d),
                         jnp.tile(meta_tokens.astype(x.dtype), (bsz, 1)),
                         jnp.zeros((n_pad, d), x.dtype)], axis=0)

    for i in range(DEPTH):
        j = i // 2
        if i % 2 == 0:
            lam0 = _lambda_init(i)
            qt, k, vt = _project(h, diff_w_qkv[j])
            dtab, mtab = _diff_tables(rel_bias)
            subln = jnp.broadcast_to(diff_subln[j].astype(F32)[:, None], (2 * HEAD_DIM, ATT_TILE))
            ot, ot_tail = _attention(qt, k, vt, bsz, seq, fox=False, lambda_init=lam0,
                                     extras=(dtab, mtab, diff_lambda[j].astype(F32), subln))
            w_o = diff_w_o[j]
        else:
            w_in = fox_w_in[j]
            qt, k, vt, lft = _project(h, w_in[:, :3 * d], w_in[:, 3 * d:].T.astype(BF16),
                                      fox_b_f[j].astype(F32)[:, None])
            cmeta, kaug = _fox_cumsum(lft, k, bsz, seq)
            ot, ot_tail = _attention(qt, k, vt, bsz, seq, fox=True,
                                     extras=(cmeta[..., None],), kaug=kaug)
            w_o = fox_w_o[j]

        n_router = N_GROUPS + N_EXPERTS
        wr = jnp.concatenate([router_group_w[i], router_expert_w[i].reshape(d, N_EXPERTS)], axis=1)
        wr = jnp.pad(wr, ((0, 0), (0, LANES - n_router))).astype(BF16)
        br = jnp.concatenate([router_group_b[i], router_expert_b[i].reshape(N_EXPERTS)])
        br = jnp.pad(br, (0, LANES - n_router)).astype(F32)[None, :]
        h1, route, route_t, counts = _mix_and_route(
            ot, ot_tail, h, w_o.astype(BF16), ln_mix_g[i][None, :], ln_mix_b[i][None, :], wr, br)
        n_out = rr if i == DEPTH - 1 else r
        h = _moe(h1, route, route_t, counts, i, expert_w_gate, expert_w_up, expert_w_down,
                 ln_ffn_g[i][None, :], ln_ffn_b[i][None, :], n_out)

    return h.reshape(bsz, seq, d)
```

```python
import functools
import math

import numpy as np
import jax
import jax.numpy as jnp
from jax import lax
from jax.experimental import pallas as pl
from jax.experimental.pallas import tpu as pltpu

F32 = jnp.float32
BF16 = jnp.bfloat16

N_META = 16
HEAD_DIM = 64
N_PAIRS = 8
LANES = 128
N_BUCKETS = 32
MAX_EXACT = 16
MAX_DISTANCE = 128
N_GROUPS = 4
EXPERTS_PER_GROUP = 8
N_EXPERTS = 32
DEPTH = 2
DN_ALPHA = (2 * DEPTH) ** 0.25
LN_EPS = 1e-5
NEG_INF = -1e30
CK_PIECES = 3
LOG2E = math.log2(math.e)
ONES_ROWS = 16

ROW_TILE = 512
ATT_TILE = 512
PAIR_GROUP = 2
EXP_TILE = 512
DMA_UNROLL = 8
VMEM_LIMIT = 48 * 1024 * 1024

_NT = (((1,), (1,)), ((), ()))
_TN = (((0,), (0,)), ((), ()))


def _lambda_init(layer_idx):
    return 0.8 - 0.6 * math.exp(-0.3 * layer_idx)


def _bucket_table(n):
    d = np.arange(n)
    nf = np.maximum(d, 1).astype(np.float64)
    large = MAX_EXACT + (np.log(nf / MAX_EXACT) / math.log(MAX_DISTANCE / MAX_EXACT)
                         * (N_BUCKETS - MAX_EXACT)).astype(np.int64)
    large = np.minimum(large, N_BUCKETS - 1)
    return np.where(d < MAX_EXACT, d, large).astype(np.int32)


def _proj_body(x_ref, wqt_ref, wk_ref, wvt_ref, qt_ref, k_ref, vt_ref):
    xb = x_ref[...].astype(BF16)
    qt = lax.dot_general(wqt_ref[...], xb, _NT, preferred_element_type=F32) * (HEAD_DIM ** -0.5 * LOG2E)
    kk = jnp.dot(xb, wk_ref[...], preferred_element_type=F32)
    vt = lax.dot_general(wvt_ref[...], xb, _NT, preferred_element_type=F32)
    for j in range(N_PAIRS):
        sl = slice(j * LANES, (j + 1) * LANES)
        qt_ref[j] = qt[sl, :].astype(BF16)
        k_ref[j] = kk[:, sl].astype(BF16)
        vt_ref[j] = vt[sl, :].astype(BF16)
    return xb


def _proj_kernel(x_ref, wqt_ref, wk_ref, wvt_ref, qt_ref, k_ref, vt_ref):
    _proj_body(x_ref, wqt_ref, wk_ref, wvt_ref, qt_ref, k_ref, vt_ref)


def _proj_fox_kernel(x_ref, wqt_ref, wk_ref, wvt_ref, wf_ref, bf_ref,
                     qt_ref, k_ref, vt_ref, lft_ref):
    xb = _proj_body(x_ref, wqt_ref, wk_ref, wvt_ref, qt_ref, k_ref, vt_ref)
    z = lax.dot_general(wf_ref[...], xb, _NT, preferred_element_type=F32) + bf_ref[...]
    lft_ref[...] = jnp.minimum(z, 0.0) - jnp.log1p(jnp.exp(-jnp.abs(z)))


def _project(h, w_qkv, wf_t=None, bf_col=None):
    r, d = h.shape
    tm = ROW_TILE
    wqt = w_qkv[:, :d].T.astype(BF16)
    wk = w_qkv[:, d:2 * d].astype(BF16)
    wvt = w_qkv[:, 2 * d:3 * d].T.astype(BF16)
    t_shape = jax.ShapeDtypeStruct((N_PAIRS, LANES, r), BF16)
    k_shape = jax.ShapeDtypeStruct((N_PAIRS, r, LANES), BF16)
    t_spec = pl.BlockSpec((N_PAIRS, LANES, tm), lambda i: (0, 0, i))
    k_spec = pl.BlockSpec((N_PAIRS, tm, LANES), lambda i: (0, i, 0))
    w_spec = pl.BlockSpec((d, d), lambda i: (0, 0))
    in_specs = [pl.BlockSpec((tm, d), lambda i: (i, 0)), w_spec, w_spec, w_spec]
    params = pltpu.CompilerParams(dimension_semantics=("arbitrary",),
                                  vmem_limit_bytes=VMEM_LIMIT)
    if wf_t is None:
        return pl.pallas_call(
            _proj_kernel, grid=(r // tm,), in_specs=in_specs,
            out_specs=[t_spec, k_spec, t_spec], out_shape=[t_shape, k_shape, t_shape],
            compiler_params=params, name="proj_diff")(h, wqt, wk, wvt)
    nh = wf_t.shape[0]
    in_specs += [pl.BlockSpec((nh, d), lambda i: (0, 0)),
                 pl.BlockSpec((nh, 1), lambda i: (0, 0))]
    return pl.pallas_call(
        _proj_fox_kernel, grid=(r // tm,), in_specs=in_specs,
        out_specs=[t_spec, k_spec, t_spec, pl.BlockSpec((nh, tm), lambda i: (0, i))],
        out_shape=[t_shape, k_shape, t_shape, jax.ShapeDtypeStruct((nh, r), F32)],
        compiler_params=params, name="proj_fox")(h, wqt, wk, wvt, wf_t, bf_col)


def _ck_lane(c):
    return HEAD_DIM if c == 0 else 0


def _placement_matrices(nh):
    e = np.zeros((nh, CK_PIECES * LANES, LANES), np.float32)
    for h in range(nh):
        for p in range(CK_PIECES):
            e[h, p * LANES + h, _ck_lane(h % 2) + p] = 1.0
    return e


def _cumsum_kernel(lfm_ref, lft_ref, k_ref, place_ref, cmeta_ref, kaug_ref, carry_ref):
    c = pl.program_id(1)
    nh, tc = lft_ref.shape

    @pl.when(c == 0)
    def _():
        lm = lfm_ref[0]
        n = lm.shape[1]
        tri = (lax.broadcasted_iota(jnp.int32, (n, n), 0)
               <= lax.broadcasted_iota(jnp.int32, (n, n), 1)).astype(F32)
        cm = jnp.dot(lm, tri, preferred_element_type=F32, precision=lax.Precision.HIGHEST)
        cmeta_ref[0] = cm * LOG2E
        carry_ref[...] = cm[:, n - 1:n]

    tri = (lax.broadcasted_iota(jnp.int32, (tc, tc), 0)
           <= lax.broadcasted_iota(jnp.int32, (tc, tc), 1)).astype(F32)
    cs = jnp.dot(lft_ref[...], tri, preferred_element_type=F32,
                 precision=lax.Precision.HIGHEST) + carry_ref[...]
    carry_ref[...] = cs[:, tc - 1:tc]

    cst = jnp.concatenate([cs * LOG2E, jnp.zeros((LANES - nh, tc), F32)], axis=0).T
    hi = cst.astype(BF16)
    rem = cst - hi.astype(F32)
    mid = rem.astype(BF16)
    lo = (rem - mid.astype(F32)).astype(BF16)
    pieces = jnp.concatenate([hi, mid, lo], axis=1)
    lane = lax.broadcasted_iota(jnp.int32, (tc, LANES), 1)
    for h in range(nh):
        ck = jnp.dot(pieces, place_ref[h], preferred_element_type=F32).astype(BF16)
        own = (lane < HEAD_DIM) if h % 2 == 0 else (lane >= HEAD_DIM)
        kaug_ref[h] = jnp.where(own, k_ref[h // 2], ck)


def _fox_cumsum(lft, k, bsz, seq):
    nh, _ = lft.shape
    rr = bsz * seq
    tc = ROW_TILE
    nc = seq // tc
    lf_meta = lft[:, rr:rr + bsz * N_META].reshape(nh, bsz, N_META).transpose(1, 0, 2)
    place = jnp.asarray(_placement_matrices(nh), BF16)
    return pl.pallas_call(
        _cumsum_kernel, grid=(bsz, nc),
        in_specs=[pl.BlockSpec((1, nh, N_META), lambda b, c: (b, 0, 0)),
                  pl.BlockSpec((nh, tc), lambda b, c: (0, b * nc + c)),
                  pl.BlockSpec((N_PAIRS, tc, LANES), lambda b, c: (0, b * nc + c, 0)),
                  pl.BlockSpec(place.shape, lambda b, c: (0, 0, 0))],
        out_specs=[pl.BlockSpec((1, nh, N_META), lambda b, c: (b, 0, 0)),
                   pl.BlockSpec((nh, tc, LANES), lambda b, c: (0, b * nc + c, 0))],
        out_shape=[jax.ShapeDtypeStruct((bsz, nh, N_META), F32),
                   jax.ShapeDtypeStruct((nh, rr, LANES), BF16)],
        scratch_shapes=[pltpu.VMEM((nh, 1), F32)],
        compiler_params=pltpu.CompilerParams(dimension_semantics=("arbitrary", "arbitrary"),
                                             vmem_limit_bytes=VMEM_LIMIT),
        name="fox_cumsum")(lf_meta, lft, k, place)


def _head_q(qt, c, fox):
    row = lax.broadcasted_iota(jnp.int32, qt.shape, 0)
    own = (row < HEAD_DIM) if c == 0 else (row >= HEAD_DIM)
    if fox:
        lo = _ck_lane(c)
        fill = jnp.where(jnp.logical_and(row >= lo, row < lo + CK_PIECES), -1.0, 0.0).astype(qt.dtype)
    else:
        fill = jnp.zeros_like(qt)
    return jnp.where(own, qt, fill)


def _head_k(kp, c):
    lane = lax.broadcasted_iota(jnp.int32, kp.shape, 1)
    own = (lane < HEAD_DIM) if c == 0 else (lane >= HEAD_DIM)
    return jnp.where(own, kp, jnp.zeros_like(kp))


def _diag_bias(d0, d1, n):
    zero = jnp.zeros_like(d0)
    neg = jnp.full_like(d0, NEG_INF)
    rows = []
    for r in range(n):
        blocks = []
        for c in range(n):
            if r > c:
                blocks.append(neg)
            elif r == c:
                blocks.append(d0)
            elif r == c - 1 and d1 is not None:
                blocks.append(d1)
            else:
                blocks.append(zero)
        rows.append(jnp.concatenate(blocks, axis=1))
    return jnp.concatenate(rows, axis=0)


def _causal_block():
    key = lax.broadcasted_iota(jnp.int32, (LANES, LANES), 0)
    qry = lax.broadcasted_iota(jnp.int32, (LANES, LANES), 1)
    return jnp.where(key > qry, NEG_INF, 0.0).astype(F32)


def _diff_lambda(lam_ref, lambda_init):
    lam = lam_ref[...]
    a = jnp.sum(lam[0:1] * lam[1:2], axis=1, keepdims=True)
    b = jnp.sum(lam[2:3] * lam[3:4], axis=1, keepdims=True)
    return jnp.exp(a) - jnp.exp(b) + lambda_init


def _head_v(vt, c, fox):
    if fox:
        row = lax.broadcasted_iota(jnp.int32, vt.shape, 0)
        own = (row < HEAD_DIM) if c == 0 else (row >= HEAD_DIM)
        return jnp.where(own, vt, jnp.ones_like(vt))
    return jnp.concatenate([vt, jnp.ones((ONES_ROWS, vt.shape[1]), vt.dtype)], axis=0)


def _denominator(acc, c, fox):
    r = _ck_lane(c) if fox else 2 * HEAD_DIM
    return acc[r:r + 1, :]


def _diff_finish(acc0, acc1, lam_full, subln, lambda_init):
    nv = 2 * HEAD_DIM
    o = (acc0[:nv] / _denominator(acc0, 0, False)
         - lam_full * (acc1[:nv] / _denominator(acc1, 1, False)))
    ms = jnp.mean(o * o, axis=0, keepdims=True)
    return o * lax.rsqrt(ms + LN_EPS) * subln * (1.0 - lambda_init)


def _fox_finish(acc0, acc1):
    row = lax.broadcasted_iota(jnp.int32, acc0.shape, 0)
    return jnp.where(row < HEAD_DIM, acc0 / _denominator(acc0, 0, True),
                     acc1 / _denominator(acc1, 1, True))


def _attn_kernel(qmap_ref, kmap_ref, qt_ref, k_ref, vt_ref, km_ref, vmt_ref, *rest,
                 fox, lambda_init):
    if fox:
        ckm_ref, ot_ref = rest[:2]
    else:
        dtab_ref, mtab_ref, lam_ref, subln_ref, ot_ref = rest[:5]
    m_sc, acc_sc, s_sc = rest[-6:-4], rest[-4:-2], rest[-2:]
    t = pl.program_id(1)
    qi = qmap_ref[t]
    ki = kmap_ref[t]
    tq = qt_ref.shape[2]
    nblk = tq // LANES

    def init_from_meta(j, c, qa, near):
        mi = 2 * j + c
        km = km_ref[j]
        if fox:
            s = jnp.dot(_head_k(km, c), qa, preferred_element_type=F32) - ckm_ref[0, mi]
        else:
            s = jnp.dot(km, qa, preferred_element_type=F32)
            if near:
                s = s + jnp.concatenate(
                    [mtab_ref[mi], jnp.zeros((N_META, tq - LANES), F32)], axis=1)
        m = jnp.max(s, axis=0, keepdims=True)
        p = jnp.exp2(s - m)
        m_sc[c][j] = m
        acc_sc[c][j] = jnp.dot(_head_v(vmt_ref[0, j], c, fox), p.astype(BF16),
                               preferred_element_type=F32)

    def tile_scores(j, c, qa, diag):
        mi = 2 * j + c
        kk = k_ref[mi] if fox else k_ref[j]
        s = jnp.dot(kk, qa, preferred_element_type=F32)
        if diag:
            if fox:
                s = s + _diag_bias(_causal_block(), None, nblk)
            else:
                s = s + _diag_bias(dtab_ref[mi, 0], dtab_ref[mi, 1], nblk)
        elif not fox:
            near = jnp.where(ki == qi - 1, 1.0, 0.0).astype(F32)
            corner = s[tq - LANES:, :LANES] + near * dtab_ref[mi, 1]
            bottom = jnp.concatenate([corner, s[tq - LANES:, LANES:]], axis=1)
            s = jnp.concatenate([s[:tq - LANES], bottom], axis=0)
        return s

    def tile_probs(j, c, s):
        m_prev = m_sc[c][j]
        m_new = jnp.maximum(m_prev, jnp.max(s, axis=0, keepdims=True))
        alpha = jnp.exp2(m_prev - m_new)
        p = jnp.exp2(s - m_new)
        m_sc[c][j] = m_new
        return alpha, p.astype(BF16)

    def tile_values(j, c, alpha, p):
        acc_sc[c][j] = alpha * acc_sc[c][j] + jnp.dot(_head_v(vt_ref[j], c, fox), p,
                                                      preferred_element_type=F32)

    def meta_init(near):
        def pair(j, carry):
            qt = qt_ref[j]
            for c in range(2):
                init_from_meta(j, c, _head_q(qt, c, fox), near)
            return carry
        lax.fori_loop(0, N_PAIRS, pair, 0)

    def run(diag):
        n_groups = N_PAIRS // PAIR_GROUP

        def scores(g):
            for u in range(PAIR_GROUP):
                j = g * PAIR_GROUP + u
                qt = qt_ref[j]
                for c in range(2):
                    s_sc[g % 2][2 * u + c] = tile_scores(j, c, _head_q(qt, c, fox), diag)

        def finish_group(g):
            for u in range(PAIR_GROUP):
                j = g * PAIR_GROUP + u
                ap = [tile_probs(j, c, s_sc[g % 2][2 * u + c]) for c in range(2)]
                for c in range(2):
                    tile_values(j, c, *ap[c])
                if diag:
                    if fox:
                        o = _fox_finish(acc_sc[0][j], acc_sc[1][j])
                    else:
                        o = _diff_finish(acc_sc[0][j], acc_sc[1][j],
                                         _diff_lambda(lam_ref, lambda_init),
                                         subln_ref[...], lambda_init)
                    ot_ref[j] = o.astype(ot_ref.dtype)

        scores(0)
        for g in range(n_groups):
            if g + 1 < n_groups:
                scores(g + 1)
            finish_group(g)

    @pl.when(ki == 0)
    def _():
        if fox:
            meta_init(False)
        else:
            @pl.when(qi == 0)
            def _():
                meta_init(True)

            @pl.when(qi > 0)
            def _():
                meta_init(False)

    @pl.when(ki == qi)
    def _():
        run(True)

    @pl.when(ki < qi)
    def _():
        run(False)


def _attn_meta_kernel(qmt_ref, k_ref, vmt_ref, *rest, fox, lambda_init):
    if fox:
        ckm_ref, ot_ref = rest
    else:
        dtab_ref, lam_ref, subln_ref, ot_ref = rest
    causal = _causal_block()[:N_META, :N_META]

    def pair(j, carry):
        qt = qmt_ref[0, j]
        km = k_ref[j]
        acc = []
        for c in range(2):
            mi = 2 * j + c
            qa = _head_q(qt, c, fox)
            if fox:
                s = (jnp.dot(_head_k(km, c), qa, preferred_element_type=F32)
                     - ckm_ref[0, mi] + causal)
            else:
                s = jnp.dot(km, qa, preferred_element_type=F32) + dtab_ref[mi, 0][:N_META, :N_META]
            m = jnp.max(s, axis=0, keepdims=True)
            p = jnp.exp2(s - m)
            acc.append(jnp.dot(_head_v(vmt_ref[0, j], c, fox), p.astype(BF16),
                               preferred_element_type=F32))
        if fox:
            o = _fox_finish(acc[0], acc[1])
        else:
            o = _diff_finish(acc[0], acc[1], _diff_lambda(lam_ref, lambda_init),
                             subln_ref[...][:, :N_META], lambda_init)
        ot_ref[0, j] = o.astype(ot_ref.dtype)
        return carry

    lax.fori_loop(0, N_PAIRS, pair, 0)


def _meta_columns(xt, bsz, rr):
    cols = xt[:, :, rr:rr + bsz * N_META]
    return cols.reshape(N_PAIRS, LANES, bsz, N_META).transpose(2, 0, 1, 3)


def _attention(qt, k, vt, bsz, seq, *, fox, lambda_init=0.0, extras=(), kaug=None):
    rr = bsz * seq
    r = qt.shape[2]
    tq = ATT_TILE
    nq = seq // tq
    qmap = np.concatenate([np.full(i + 1, i) for i in range(nq)]).astype(np.int32)
    kmap = np.concatenate([np.arange(i + 1) for i in range(nq)]).astype(np.int32)
    nsteps = int(qmap.shape[0])
    meta_blk = rr // N_META
    qmt = _meta_columns(qt, bsz, rr)
    vmt = _meta_columns(vt, bsz, rr)
    keys = kaug if fox else k
    nkh = keys.shape[0]

    tile_qt = pl.BlockSpec((N_PAIRS, LANES, tq), lambda b, t, qm, km: (0, 0, b * nq + qm[t]))
    tile_vt = pl.BlockSpec((N_PAIRS, LANES, tq), lambda b, t, qm, km: (0, 0, b * nq + km[t]))
    tile_k = pl.BlockSpec((nkh, tq, LANES), lambda b, t, qm, km: (0, b * nq + km[t], 0))
    meta_k = pl.BlockSpec((N_PAIRS, N_META, LANES), lambda b, t, qm, km: (0, meta_blk + b, 0))
    meta_vt = pl.BlockSpec((1, N_PAIRS, LANES, N_META), lambda b, t, qm, km: (b, 0, 0, 0))

    if fox:
        cmeta = extras[0]
        extra_specs = [pl.BlockSpec((1,) + cmeta.shape[1:], lambda b, t, qm, km: (b, 0, 0, 0))]
    else:
        extra_specs = [pl.BlockSpec(a.shape, functools.partial(lambda nd, b, t, qm, km: (0,) * nd,
                                                               a.ndim)) for a in extras]

    ot = pl.pallas_call(
        functools.partial(_attn_kernel, fox=fox, lambda_init=lambda_init),
        grid_spec=pltpu.PrefetchScalarGridSpec(
            num_scalar_prefetch=2, grid=(bsz, nsteps),
            in_specs=[tile_qt, tile_k, tile_vt, meta_k, meta_vt] + extra_specs,
            out_specs=tile_qt,
            scratch_shapes=[pltpu.VMEM((N_PAIRS, 1, tq), F32)] * 2
            + [pltpu.VMEM((N_PAIRS, LANES if fox else LANES + ONES_ROWS, tq), F32)] * 2
            + [pltpu.VMEM((2 * PAIR_GROUP, tq, tq), F32)] * 2),
        out_shape=jax.ShapeDtypeStruct((N_PAIRS, LANES, rr), BF16),
        compiler_params=pltpu.CompilerParams(
            dimension_semantics=("arbitrary", "arbitrary"), vmem_limit_bytes=VMEM_LIMIT),
        name="attn_fox" if fox else "attn_diff",
    )(jnp.asarray(qmap), jnp.asarray(kmap), qt, keys, vt, k, vmt, *extras)

    per_b = lambda b: (b, 0, 0, 0)
    if fox:
        m_extras = (extras[0],)
        m_specs = [pl.BlockSpec((1,) + extras[0].shape[1:], per_b)]
    else:
        m_extras = (extras[0], extras[2], extras[3])
        m_specs = [pl.BlockSpec(a.shape, functools.partial(lambda nd, b: (0,) * nd, a.ndim))
                   for a in m_extras]
    ot_meta = pl.pallas_call(
        functools.partial(_attn_meta_kernel, fox=fox, lambda_init=lambda_init),
        grid=(bsz,),
        in_specs=[pl.BlockSpec((1, N_PAIRS, LANES, N_META), per_b),
                  pl.BlockSpec((N_PAIRS, N_META, LANES), lambda b: (0, meta_blk + b, 0)),
                  pl.BlockSpec((1, N_PAIRS, LANES, N_META), per_b)] + m_specs,
        out_specs=pl.BlockSpec((1, N_PAIRS, LANES, N_META), per_b),
        out_shape=jax.ShapeDtypeStruct((bsz, N_PAIRS, LANES, N_META), BF16),
        compiler_params=pltpu.CompilerParams(dimension_semantics=("arbitrary",)),
        name="attn_fox_meta" if fox else "attn_diff_meta",
    )(qmt, k, vmt, *m_extras)
    tail = ot_meta.transpose(1, 2, 0, 3).reshape(N_PAIRS, LANES, bsz * N_META)
    tail = jnp.pad(tail, ((0, 0), (0, 0), (0, r - rr - bsz * N_META)))
    return ot, tail


def _layer_norm(y, g, b):
    mu = jnp.mean(y, axis=1, keepdims=True)
    yc = y - mu
    var = jnp.mean(yc * yc, axis=1, keepdims=True)
    return yc * lax.rsqrt(var + LN_EPS) * g + b


def _mix_router_kernel(ot_ref, otm_ref, h_ref, wo_ref, g_ref, b_ref, wr_ref, br_ref,
                       h1_ref, route_ref, route_t_ref, counts_ref, carry_ref):
    i = pl.program_id(0)
    tm = h_ref.shape[0]

    @pl.when(i == 0)
    def _():
        carry_ref[...] = jnp.zeros_like(carry_ref)

    is_tail = i == pl.num_programs(0) - 1
    ot = jnp.concatenate([jnp.where(is_tail, otm_ref[j], ot_ref[j]) for j in range(N_PAIRS)],
                         axis=0)
    mix = lax.dot_general(ot, wo_ref[...], _TN, preferred_element_type=F32)
    h1 = _layer_norm(DN_ALPHA * h_ref[...] + mix, g_ref[...], b_ref[...])
    h1_ref[...] = h1

    logits = jnp.dot(h1.astype(BF16), wr_ref[...], preferred_element_type=F32) + br_ref[...]
    lane = lax.broadcasted_iota(jnp.int32, logits.shape, 1)
    lane_f = lane.astype(F32)
    big = float(LANES)
    g_mask = lane < N_GROUPS
    lg = jnp.where(g_mask, logits, NEG_INF)
    gmax = jnp.max(lg, axis=1, keepdims=True)
    gsel = jnp.min(jnp.where(lg == gmax, lane_f, big), axis=1, keepdims=True)
    p_group = 1.0 / jnp.sum(jnp.where(g_mask, jnp.exp(lg - gmax), 0.0), axis=1, keepdims=True)
    e_lo = N_GROUPS + EXPERTS_PER_GROUP * gsel
    e_mask = jnp.logical_and(lane_f >= e_lo, lane_f < e_lo + EXPERTS_PER_GROUP)
    le = jnp.where(e_mask, logits, NEG_INF)
    v1 = jnp.max(le, axis=1, keepdims=True)
    i1 = jnp.min(jnp.where(jnp.logical_and(le == v1, e_mask), lane_f, big),
                 axis=1, keepdims=True)
    e_mask2 = jnp.logical_and(e_mask, lane_f != i1)
    le2 = jnp.where(e_mask2, logits, NEG_INF)
    v2 = jnp.max(le2, axis=1, keepdims=True)
    i2 = jnp.min(jnp.where(jnp.logical_and(le2 == v2, e_mask2), lane_f, big),
                 axis=1, keepdims=True)
    ex = jnp.exp(v2 - v1)
    gate0 = p_group * (1.0 / (1.0 + ex))
    gate1 = p_group * (ex / (1.0 + ex))
    e0 = i1 - N_GROUPS
    e1 = i2 - N_GROUPS

    onehot = jnp.logical_or(lane_f == e0, lane_f == e1)
    oh_bf = jnp.where(onehot, 1.0, 0.0).astype(BF16)
    ltri = jnp.where(lax.broadcasted_iota(jnp.int32, (tm, tm), 1)
                     < lax.broadcasted_iota(jnp.int32, (tm, tm), 0), 1.0, 0.0).astype(BF16)
    before = jnp.dot(ltri, oh_bf, preferred_element_type=F32) + carry_ref[...]
    r0 = jnp.sum(jnp.where(lane_f == e0, before, 0.0), axis=1, keepdims=True)
    r1 = jnp.sum(jnp.where(lane_f == e1, before, 0.0), axis=1, keepdims=True)
    carry = carry_ref[...] + jnp.sum(jnp.where(onehot, 1.0, 0.0), axis=0, keepdims=True)
    carry_ref[...] = carry
    counts_ref[...] = carry

    route = jnp.zeros_like(logits)
    for col, val in enumerate((e0, e1, r0, r1, gate0, gate1)):
        route = jnp.where(lane == col, val, route)
    route_ref[...] = route
    route_t_ref[...] = route.T[:route_t_ref.shape[0], :]


def _mix_and_route(ot, ot_tail, h, wo_bf, ln_g, ln_b, wr_bf, br):
    r, d = h.shape
    tm = ROW_TILE
    assert ot_tail.shape[2] == tm and ot.shape[2] == r - tm
    last_real = ot.shape[2] // tm - 1
    row = lambda i: (i, 0)
    const = lambda i: (0, 0)
    return pl.pallas_call(
        _mix_router_kernel, grid=(r // tm,),
        in_specs=[pl.BlockSpec((N_PAIRS, LANES, tm), lambda i: (0, 0, jnp.minimum(i, last_real))),
                  pl.BlockSpec((N_PAIRS, LANES, tm), lambda i: (0, 0, 0)),
                  pl.BlockSpec((tm, d), row),
                  pl.BlockSpec((d, d), const),
                  pl.BlockSpec((1, d), const), pl.BlockSpec((1, d), const),
                  pl.BlockSpec((d, LANES), const), pl.BlockSpec((1, LANES), const)],
        out_specs=[pl.BlockSpec((tm, d), row), pl.BlockSpec((tm, LANES), row),
                   pl.BlockSpec((8, tm), lambda i: (0, i)), pl.BlockSpec((1, LANES), const)],
        out_shape=[jax.ShapeDtypeStruct((r, d), F32), jax.ShapeDtypeStruct((r, LANES), F32),
                   jax.ShapeDtypeStruct((8, r), F32), jax.ShapeDtypeStruct((1, LANES), F32)],
        scratch_shapes=[pltpu.VMEM((1, LANES), F32)],
        compiler_params=pltpu.CompilerParams(dimension_semantics=("arbitrary",),
                                             vmem_limit_bytes=VMEM_LIMIT),
        name="mix_router")(ot, ot_tail, h, wo_bf, ln_g, ln_b, wr_bf, br)


def _row_copy(src_ref, src_row, dst_ref, dst_row, sem):
    return pltpu.make_async_copy(src_ref.at[pl.ds(src_row, 1), :],
                                 dst_ref.at[pl.ds(dst_row, 1), :], sem)


def _fetch_slots(slots_ref, slot_smem, idx_sem):
    i = pl.program_id(0)
    n = slot_smem.shape[0] // 2
    cur = lax.rem(i, 2)

    def copy(step, buf):
        dst = slot_smem.at[pl.ds(pl.multiple_of(buf * n, n), n)]
        return pltpu.make_async_copy(slots_ref.at[step], dst, idx_sem.at[buf])

    @pl.when(i == 0)
    def _():
        copy(0, 0).start()

    @pl.when(i + 1 < pl.num_programs(0))
    def _():
        copy(i + 1, 1 - cur).start()

    copy(i, cur).wait()
    return cur * n


def _dispatch_kernel(slots_ref, h_ref, xs_in_ref, xs_ref, slot_smem, sem, idx_sem):
    del xs_in_ref
    tm = h_ref.shape[0]
    base = _fetch_slots(slots_ref, slot_smem, idx_sem)

    def start(t, carry):
        for c in range(2):
            _row_copy(h_ref, t, xs_ref, slot_smem[base + c * tm + t], sem).start()
        return carry

    def wait(t, carry):
        for c in range(2):
            _row_copy(h_ref, t, xs_ref, slot_smem[base + c * tm + t], sem).wait()
        return carry

    lax.fori_loop(0, tm, start, 0, unroll=DMA_UNROLL)
    lax.fori_loop(0, tm, wait, 0, unroll=DMA_UNROLL)


def _dispatch(h1, slots, n_rows):
    r, d = h1.shape
    tm = ROW_TILE
    xs0 = jnp.zeros((n_rows, d), F32)
    any_spec = pl.BlockSpec(memory_space=pl.ANY)
    return pl.pallas_call(
        _dispatch_kernel, grid=(r // tm,),
        in_specs=[any_spec, pl.BlockSpec((tm, d), lambda i: (i, 0)), any_spec],
        out_specs=any_spec,
        out_shape=jax.ShapeDtypeStruct((n_rows, d), F32),
        scratch_shapes=[pltpu.SMEM((4 * tm,), jnp.int32), pltpu.SemaphoreType.DMA,
                        pltpu.SemaphoreType.DMA((2,))],
        input_output_aliases={2: 0},
        compiler_params=pltpu.CompilerParams(dimension_semantics=("arbitrary",)),
        name="moe_dispatch")(slots, h1, xs0)


def _expert_kernel(tile_e_ref, n_used_ref, x_ref, wg_ref, wu_ref, wd_ref, y_ref):
    del tile_e_ref

    @pl.when(pl.program_id(0) < n_used_ref[0])
    def _():
        xb = x_ref[...].astype(BF16)
        g = jnp.dot(xb, wg_ref[0, 0].astype(BF16), preferred_element_type=F32)
        u = jnp.dot(xb, wu_ref[0, 0].astype(BF16), preferred_element_type=F32)
        a = (g * jax.nn.sigmoid(g) * u).astype(BF16)
        y_ref[...] = jnp.dot(a, wd_ref[0, 0].astype(BF16), preferred_element_type=F32)

    @pl.when(pl.program_id(0) >= n_used_ref[0])
    def _():
        y_ref[...] = jnp.zeros_like(y_ref)


def _experts(xs, tile_e, n_used, layer, w_gate, w_up, w_down):
    p, d = xs.shape
    te = EXP_TILE
    de = w_gate.shape[3]
    row = lambda i, te_ref, nu_ref: (jnp.minimum(i, nu_ref[0] - 1), 0)
    wsel = lambda i, te_ref, nu_ref: (layer, te_ref[i], 0, 0)
    return pl.pallas_call(
        _expert_kernel,
        grid_spec=pltpu.PrefetchScalarGridSpec(
            num_scalar_prefetch=2, grid=(p // te,),
            in_specs=[pl.BlockSpec((te, d), row),
                      pl.BlockSpec((1, 1, d, de), wsel), pl.BlockSpec((1, 1, d, de), wsel),
                      pl.BlockSpec((1, 1, de, d), wsel)],
            out_specs=pl.BlockSpec((te, d), lambda i, te_ref, nu_ref: (i, 0))),
        out_shape=jax.ShapeDtypeStruct((p, d), F32),
        compiler_params=pltpu.CompilerParams(dimension_semantics=("arbitrary",),
                                             vmem_limit_bytes=VMEM_LIMIT),
        name="moe_experts")(tile_e, n_used, xs, w_gate, w_up, w_down)


def _combine_kernel(slots_ref, ys_ref, route_ref, h_ref, g_ref, b_ref, out_ref,
                    slot_smem, y0_ref, y1_ref, sem, idx_sem):
    tm = h_ref.shape[0]
    idx_copy = pltpu.make_async_copy(slots_ref.at[pl.program_id(0)], slot_smem, idx_sem)
    idx_copy.start()
    idx_copy.wait()
    base = 0
    bufs = (y0_ref, y1_ref)

    def start(t, carry):
        for c in range(2):
            _row_copy(ys_ref, slot_smem[base + c * tm + t], bufs[c], t, sem).start()
        return carry

    def wait(t, carry):
        for c in range(2):
            _row_copy(ys_ref, slot_smem[base + c * tm + t], bufs[c], t, sem).wait()
        return carry

    lax.fori_loop(0, tm, start, 0, unroll=DMA_UNROLL)
    lax.fori_loop(0, tm, wait, 0, unroll=DMA_UNROLL)
    route = route_ref[...]
    ffn = route[:, 4:5] * y0_ref[...] + route[:, 5:6] * y1_ref[...]
    out_ref[...] = _layer_norm(DN_ALPHA * h_ref[...] + ffn, g_ref[...], b_ref[...])


def _combine(ys, slots, route, h1, ln_g, ln_b, n_out_rows):
    r, d = h1.shape
    tm = ROW_TILE
    row = lambda i: (i, 0)
    const = lambda i: (0, 0)
    any_spec = pl.BlockSpec(memory_space=pl.ANY)
    return pl.pallas_call(
        _combine_kernel, grid=(n_out_rows // tm,),
        in_specs=[any_spec, any_spec, pl.BlockSpec((tm, LANES), row),
                  pl.BlockSpec((tm, d), row), pl.BlockSpec((1, d), const),
                  pl.BlockSpec((1, d), const)],
        out_specs=pl.BlockSpec((tm, d), row),
        out_shape=jax.ShapeDtypeStruct((n_out_rows, d), F32),
        scratch_shapes=[pltpu.SMEM((2 * tm,), jnp.int32), pltpu.VMEM((tm, d), F32),
                        pltpu.VMEM((tm, d), F32), pltpu.SemaphoreType.DMA,
                        pltpu.SemaphoreType.DMA],
        compiler_params=pltpu.CompilerParams(dimension_semantics=("arbitrary",),
                                             vmem_limit_bytes=VMEM_LIMIT),
        name="moe_combine")(slots, ys, route, h1, ln_g, ln_b)


def _moe(h1, route, route_t, counts, layer, w_gate, w_up, w_down, ln_g, ln_b, n_out_rows):
    r, _ = h1.shape
    tm, te = ROW_TILE, EXP_TILE
    n_tiles = (2 * r) // te + N_EXPERTS
    cnt = counts[0, :N_EXPERTS].astype(jnp.int32)
    seg_tiles = (cnt + te - 1) // te
    seg_end = jnp.cumsum(seg_tiles)
    seg_start = (seg_end - seg_tiles) * te
    n_used = seg_end[-1:]
    tile_ids = jnp.arange(n_tiles, dtype=jnp.int32)
    tile_e = jnp.sum(tile_ids[:, None] >= seg_end[None, :], axis=1)
    last_e = jnp.sum(n_used - 1 >= seg_end)
    tile_e = jnp.minimum(tile_e, last_e).astype(jnp.int32)
    e01 = route_t[0:2].astype(jnp.int32)
    first = jnp.sum(jnp.where(e01[..., None] == jnp.arange(N_EXPERTS), seg_start, 0), axis=-1)
    slot = first + route_t[2:4].astype(jnp.int32)
    slots = slot.reshape(2, r // tm, tm).transpose(1, 0, 2).reshape(r // tm, 2 * tm)

    xs = _dispatch(h1, slots, n_tiles * te)
    ys = _experts(xs, tile_e, n_used.astype(jnp.int32), layer, w_gate, w_up, w_down)
    return _combine(ys, slots, route, h1, ln_g, ln_b, n_out_rows)


def _diff_tables(rel_bias):
    bkt = _bucket_table(2 * LANES + N_META)
    rel = (rel_bias - rel_bias[N_BUCKETS - 1:N_BUCKETS]).T * LOG2E

    def lookup(dist):
        onehot = (bkt[dist][None] == np.arange(N_BUCKETS)[:, None, None]).astype(np.float32)
        return jnp.einsum("mb,bkq->mkq", rel, jnp.asarray(onehot), precision=lax.Precision.HIGHEST)

    key = np.arange(LANES)[:, None]
    qry = np.arange(LANES)[None, :]
    d0 = jnp.where(jnp.asarray(qry >= key), lookup(np.maximum(qry - key, 0)), NEG_INF)
    d1 = lookup(LANES + qry - key)
    dtab = jnp.stack([d0, d1], axis=1)
    km = np.arange(N_META)[:, None]
    mtab = lookup(N_META + qry - km)
    return dtab.astype(F32), mtab.astype(F32)


def kernel(x, meta_tokens, rel_bias, diff_w_qkv, diff_lambda, diff_subln, diff_w_o,
           fox_w_in, fox_b_f, fox_w_o, ln_mix_g, ln_mix_b, ln_ffn_g, ln_ffn_b,
           router_group_w, router_group_b, router_expert_w, router_expert_b,
           expert_w_gate, expert_w_up, expert_w_down):
    bsz, seq, d = x.shape
    rr = bsz * seq
    r = rr + ROW_TILE
    assert seq % ATT_TILE == 0 and bsz * N_META <= ROW_TILE and d == N_PAIRS * LANES
    n_pad = r - rr - bsz * N_META
    h = jnp.concatenate([x.reshape(rr, d),
                         jnp.tile(meta_tokens.astype(x.dtype), (bsz, 1)),
                         jnp.zeros((n_pad, d), x.dtype)], axis=0)

    for i in range(DEPTH):
        j = i // 2
        if i % 2 == 0:
            lam0 = _lambda_init(i)
            qt, k, vt = _project(h, diff_w_qkv[j])
            dtab, mtab = _diff_tables(rel_bias)
            subln = jnp.broadcast_to(diff_subln[j].astype(F32)[:, None], (2 * HEAD_DIM, ATT_TILE))
            ot, ot_tail = _attention(qt, k, vt, bsz, seq, fox=False, lambda_init=lam0,
                                     extras=(dtab, mtab, diff_lambda[j].astype(F32), subln))
            w_o = diff_w_o[j]
        else:
            w_in = fox_w_in[j]
            qt, k, vt, lft = _project(h, w_in[:, :3 * d], w_in[:, 3 * d:].T.astype(BF16),
                                      fox_b_f[j].astype(F32)[:, None])
            cmeta, kaug = _fox_cumsum(lft, k, bsz, seq)
            ot, ot_tail = _attention(qt, k, vt, bsz, seq, fox=True,
                                     extras=(cmeta[..., None],), kaug=kaug)
            w_o = fox_w_o[j]

        n_router = N_GROUPS + N_EXPERTS
        wr = jnp.concatenate([router_group_w[i], router_expert_w[i].reshape(d, N_EXPERTS)], axis=1)
        wr = jnp.pad(wr, ((0, 0), (0, LANES - n_router))).astype(BF16)
        br = jnp.concatenate([router_group_b[i], router_expert_b[i].reshape(N_EXPERTS)])
        br = jnp.pad(br, (0, LANES - n_router)).astype(F32)[None, :]
        h1, route, route_t, counts = _mix_and_route(
            ot, ot_tail, h, w_o.astype(BF16), ln_mix_g[i][None, :], ln_mix_b[i][None, :], wr, br)
        n_out = rr if i == DEPTH - 1 else r
        h = _moe(h1, route, route_t, counts, i, expert_w_gate, expert_w_up, expert_w_down,
                 ln_ffn_g[i][None, :], ln_ffn_b[i][None, :], n_out)

    return h.reshape(bsz, seq, d)
```

```python
import functools
import math

import numpy as np
import jax
import jax.numpy as jnp
from jax import lax
from jax.experimental import pallas as pl
from jax.experimental.pallas import tpu as pltpu

F32 = jnp.float32
BF16 = jnp.bfloat16

N_META = 16
HEAD_DIM = 64
N_PAIRS = 8
LANES = 128
N_BUCKETS = 32
MAX_EXACT = 16
MAX_DISTANCE = 128
N_GROUPS = 4
EXPERTS_PER_GROUP = 8
N_EXPERTS = 32
DEPTH = 2
DN_ALPHA = (2 * DEPTH) ** 0.25
LN_EPS = 1e-5
NEG_INF = -1e30
CK_PIECES = 3
LOG2E = math.log2(math.e)
ONES_ROWS = 16

ROW_TILE = 512
ATT_TILE = 512
PAIR_GROUP = 2
EXP_TILE = 512
DMA_UNROLL = 8
VMEM_LIMIT = 48 * 1024 * 1024

_NT = (((1,), (1,)), ((), ()))
_TN = (((0,), (0,)), ((), ()))


def _lambda_init(layer_idx):
    return 0.8 - 0.6 * math.exp(-0.3 * layer_idx)


def _bucket_table(n):
    d = np.arange(n)
    nf = np.maximum(d, 1).astype(np.float64)
    large = MAX_EXACT + (np.log(nf / MAX_EXACT) / math.log(MAX_DISTANCE / MAX_EXACT)
                         * (N_BUCKETS - MAX_EXACT)).astype(np.int64)
    large = np.minimum(large, N_BUCKETS - 1)
    return np.where(d < MAX_EXACT, d, large).astype(np.int32)


def _proj_body(x_ref, wqt_ref, wk_ref, wvt_ref, qt_ref, k_ref, vt_ref):
    xb = x_ref[...].astype(BF16)
    qt = lax.dot_general(wqt_ref[...], xb, _NT, preferred_element_type=F32) * (HEAD_DIM ** -0.5 * LOG2E)
    kk = jnp.dot(xb, wk_ref[...], preferred_element_type=F32)
    vt = lax.dot_general(wvt_ref[...], xb, _NT, preferred_element_type=F32)
    for j in range(N_PAIRS):
        sl = slice(j * LANES, (j + 1) * LANES)
        qt_ref[j] = qt[sl, :].astype(BF16)
        k_ref[j] = kk[:, sl].astype(BF16)
        vt_ref[j] = vt[sl, :].astype(BF16)
    return xb


def _proj_kernel(x_ref, wqt_ref, wk_ref, wvt_ref, qt_ref, k_ref, vt_ref):
    _proj_body(x_ref, wqt_ref, wk_ref, wvt_ref, qt_ref, k_ref, vt_ref)


def _proj_fox_kernel(x_ref, wqt_ref, wk_ref, wvt_ref, wf_ref, bf_ref,
                     qt_ref, k_ref, vt_ref, lft_ref):
    xb = _proj_body(x_ref, wqt_ref, wk_ref, wvt_ref, qt_ref, k_ref, vt_ref)
    z = lax.dot_general(wf_ref[...], xb, _NT, preferred_element_type=F32) + bf_ref[...]
    lft_ref[...] = jnp.minimum(z, 0.0) - jnp.log1p(jnp.exp(-jnp.abs(z)))


def _project(h, w_qkv, wf_t=None, bf_col=None):
    r, d = h.shape
    tm = ROW_TILE
    wqt = w_qkv[:, :d].T.astype(BF16)
    wk = w_qkv[:, d:2 * d].astype(BF16)
    wvt = w_qkv[:, 2 * d:3 * d].T.astype(BF16)
    t_shape = jax.ShapeDtypeStruct((N_PAIRS, LANES, r), BF16)
    k_shape = jax.ShapeDtypeStruct((N_PAIRS, r, LANES), BF16)
    t_spec = pl.BlockSpec((N_PAIRS, LANES, tm), lambda i: (0, 0, i))
    k_spec = pl.BlockSpec((N_PAIRS, tm, LANES), lambda i: (0, i, 0))
    w_spec = pl.BlockSpec((d, d), lambda i: (0, 0))
    in_specs = [pl.BlockSpec((tm, d), lambda i: (i, 0)), w_spec, w_spec, w_spec]
    params = pltpu.CompilerParams(dimension_semantics=("arbitrary",),
                                  vmem_limit_bytes=VMEM_LIMIT)
    if wf_t is None:
        return pl.pallas_call(
            _proj_kernel, grid=(r // tm,), in_specs=in_specs,
            out_specs=[t_spec, k_spec, t_spec], out_shape=[t_shape, k_shape, t_shape],
            compiler_params=params, name="proj_diff")(h, wqt, wk, wvt)
    nh = wf_t.shape[0]
    in_specs += [pl.BlockSpec((nh, d), lambda i: (0, 0)),
                 pl.BlockSpec((nh, 1), lambda i: (0, 0))]
    return pl.pallas_call(
        _proj_fox_kernel, grid=(r // tm,), in_specs=in_specs,
        out_specs=[t_spec, k_spec, t_spec, pl.BlockSpec((nh, tm), lambda i: (0, i))],
        out_shape=[t_shape, k_shape, t_shape, jax.ShapeDtypeStruct((nh, r), F32)],
        compiler_params=params, name="proj_fox")(h, wqt, wk, wvt, wf_t, bf_col)


def _ck_lane(c):
    return HEAD_DIM if c == 0 else 0


def _placement_matrices(nh):
    e = np.zeros((nh, CK_PIECES * LANES, LANES), np.float32)
    for h in range(nh):
        for p in range(CK_PIECES):
            e[h, p * LANES + h, _ck_lane(h % 2) + p] = 1.0
    return e


def _cumsum_kernel(lfm_ref, lft_ref, k_ref, place_ref, cmeta_ref, kaug_ref, carry_ref):
    c = pl.program_id(1)
    nh, tc = lft_ref.shape

    @pl.when(c == 0)
    def _():
        lm = lfm_ref[0]
        n = lm.shape[1]
        tri = (lax.broadcasted_iota(jnp.int32, (n, n), 0)
               <= lax.broadcasted_iota(jnp.int32, (n, n), 1)).astype(F32)
        cm = jnp.dot(lm, tri, preferred_element_type=F32, precision=lax.Precision.HIGHEST)
        cmeta_ref[0] = cm * LOG2E
        carry_ref[...] = cm[:, n - 1:n]

    tri = (lax.broadcasted_iota(jnp.int32, (tc, tc), 0)
           <= lax.broadcasted_iota(jnp.int32, (tc, tc), 1)).astype(F32)
    cs = jnp.dot(lft_ref[...], tri, preferred_element_type=F32,
                 precision=lax.Precision.HIGHEST) + carry_ref[...]
    carry_ref[...] = cs[:, tc - 1:tc]

    cst = jnp.concatenate([cs * LOG2E, jnp.zeros((LANES - nh, tc), F32)], axis=0).T
    hi = cst.astype(BF16)
    rem = cst - hi.astype(F32)
    mid = rem.astype(BF16)
    lo = (rem - mid.astype(F32)).astype(BF16)
    pieces = jnp.concatenate([hi, mid, lo], axis=1)
    lane = lax.broadcasted_iota(jnp.int32, (tc, LANES), 1)
    for h in range(nh):
        ck = jnp.dot(pieces, place_ref[h], preferred_element_type=F32).astype(BF16)
        own = (lane < HEAD_DIM) if h % 2 == 0 else (lane >= HEAD_DIM)
        kaug_ref[h] = jnp.where(own, k_ref[h // 2], ck)


def _fox_cumsum(lft, k, bsz, seq):
    nh, _ = lft.shape
    rr = bsz * seq
    tc = ROW_TILE
    nc = seq // tc
    lf_meta = lft[:, rr:rr + bsz * N_META].reshape(nh, bsz, N_META).transpose(1, 0, 2)
    place = jnp.asarray(_placement_matrices(nh), BF16)
    return pl.pallas_call(
        _cumsum_kernel, grid=(bsz, nc),
        in_specs=[pl.BlockSpec((1, nh, N_META), lambda b, c: (b, 0, 0)),
                  pl.BlockSpec((nh, tc), lambda b, c: (0, b * nc + c)),
                  pl.BlockSpec((N_PAIRS, tc, LANES), lambda b, c: (0, b * nc + c, 0)),
                  pl.BlockSpec(place.shape, lambda b, c: (0, 0, 0))],
        out_specs=[pl.BlockSpec((1, nh, N_META), lambda b, c: (b, 0, 0)),
                   pl.BlockSpec((nh, tc, LANES), lambda b, c: (0, b * nc + c, 0))],
        out_shape=[jax.ShapeDtypeStruct((bsz, nh, N_META), F32),
                   jax.ShapeDtypeStruct((nh, rr, LANES), BF16)],
        scratch_shapes=[pltpu.VMEM((nh, 1), F32)],
        compiler_params=pltpu.CompilerParams(dimension_semantics=("arbitrary", "arbitrary"),
                                             vmem_limit_bytes=VMEM_LIMIT),
        name="fox_cumsum")(lf_meta, lft, k, place)


def _head_q(qt, c, fox):
    row = lax.broadcasted_iota(jnp.int32, qt.shape, 0)
    own = (row < HEAD_DIM) if c == 0 else (row >= HEAD_DIM)
    if fox:
        lo = _ck_lane(c)
        fill = jnp.where(jnp.logical_and(row >= lo, row < lo + CK_PIECES), -1.0, 0.0).astype(qt.dtype)
    else:
        fill = jnp.zeros_like(qt)
    return jnp.where(own, qt, fill)


def _head_k(kp, c):
    lane = lax.broadcasted_iota(jnp.int32, kp.shape, 1)
    own = (lane < HEAD_DIM) if c == 0 else (lane >= HEAD_DIM)
    return jnp.where(own, kp, jnp.zeros_like(kp))


def _diag_bias(d0, d1, n):
    zero = jnp.zeros_like(d0)
    neg = jnp.full_like(d0, NEG_INF)
    rows = []
    for r in range(n):
        blocks = []
        for c in range(n):
            if r > c:
                blocks.append(neg)
            elif r == c:
                blocks.append(d0)
            elif r == c - 1 and d1 is not None:
                blocks.append(d1)
            else:
                blocks.append(zero)
        rows.append(jnp.concatenate(blocks, axis=1))
    return jnp.concatenate(rows, axis=0)


def _causal_block():
    key = lax.broadcasted_iota(jnp.int32, (LANES, LANES), 0)
    qry = lax.broadcasted_iota(jnp.int32, (LANES, LANES), 1)
    return jnp.where(key > qry, NEG_INF, 0.0).astype(F32)


def _diff_lambda(lam_ref, lambda_init):
    lam = lam_ref[...]
    a = jnp.sum(lam[0:1] * lam[1:2], axis=1, keepdims=True)
    b = jnp.sum(lam[2:3] * lam[3:4], axis=1, keepdims=True)
    return jnp.exp(a) - jnp.exp(b) + lambda_init


def _head_v(vt, c, fox):
    if fox:
        row = lax.broadcasted_iota(jnp.int32, vt.shape, 0)
        own = (row < HEAD_DIM) if c == 0 else (row >= HEAD_DIM)
        return jnp.where(own, vt, jnp.ones_like(vt))
    return jnp.concatenate([vt, jnp.ones((ONES_ROWS, vt.shape[1]), vt.dtype)], axis=0)


def _denominator(acc, c, fox):
    r = _ck_lane(c) if fox else 2 * HEAD_DIM
    return acc[r:r + 1, :]


def _diff_finish(acc0, acc1, lam_full, subln, lambda_init):
    nv = 2 * HEAD_DIM
    o = (acc0[:nv] / _denominator(acc0, 0, False)
         - lam_full * (acc1[:nv] / _denominator(acc1, 1, False)))
    ms = jnp.mean(o * o, axis=0, keepdims=True)
    return o * lax.rsqrt(ms + LN_EPS) * subln * (1.0 - lambda_init)


def _fox_finish(acc0, acc1):
    row = lax.broadcasted_iota(jnp.int32, acc0.shape, 0)
    return jnp.where(row < HEAD_DIM, acc0 / _denominator(acc0, 0, True),
                     acc1 / _denominator(acc1, 1, True))


def _attn_kernel(qmap_ref, kmap_ref, qt_ref, k_ref, vt_ref, km_ref, vmt_ref, *rest,
                 fox, lambda_init):
    if fox:
        ckm_ref, ot_ref = rest[:2]
    else:
        dtab_ref, mtab_ref, lam_ref, subln_ref, ot_ref = rest[:5]
    m_sc, acc_sc, s_sc = rest[-6:-4], rest[-4:-2], rest[-2:]
    t = pl.program_id(1)
    qi = qmap_ref[t]
    ki = kmap_ref[t]
    tq = qt_ref.shape[2]
    nblk = tq // LANES

    def init_from_meta(j, c, qa, near):
        mi = 2 * j + c
        km = km_ref[j]
        if fox:
            s = jnp.dot(_head_k(km, c), qa, preferred_element_type=F32) - ckm_ref[0, mi]
        else:
            s = jnp.dot(km, qa, preferred_element_type=F32)
            if near:
                s = s + jnp.concatenate(
                    [mtab_ref[mi], jnp.zeros((N_META, tq - LANES), F32)], axis=1)
        m = jnp.max(s, axis=0, keepdims=True)
        p = jnp.exp2(s - m)
        m_sc[c][j] = m
        acc_sc[c][j] = jnp.dot(_head_v(vmt_ref[0, j], c, fox), p.astype(BF16),
                               preferred_element_type=F32)

    def tile_scores(j, c, qa, diag):
        mi = 2 * j + c
        kk = k_ref[mi] if fox else k_ref[j]
        s = jnp.dot(kk, qa, preferred_element_type=F32)
        if diag:
            if fox:
                s = s + _diag_bias(_causal_block(), None, nblk)
            else:
                s = s + _diag_bias(dtab_ref[mi, 0], dtab_ref[mi, 1], nblk)
        elif not fox:
            near = jnp.where(ki == qi - 1, 1.0, 0.0).astype(F32)
            corner = s[tq - LANES:, :LANES] + near * dtab_ref[mi, 1]
            bottom = jnp.concatenate([corner, s[tq - LANES:, LANES:]], axis=1)
            s = jnp.concatenate([s[:tq - LANES], bottom], axis=0)
        return s

    def tile_probs(j, c, s):
        m_prev = m_sc[c][j]
        m_new = jnp.maximum(m_prev, jnp.max(s, axis=0, keepdims=True))
        alpha = jnp.exp2(m_prev - m_new)
        p = jnp.exp2(s - m_new)
        m_sc[c][j] = m_new
        return alpha, p.astype(BF16)

    def tile_values(j, c, alpha, p):
        acc_sc[c][j] = alpha * acc_sc[c][j] + jnp.dot(_head_v(vt_ref[j], c, fox), p,
                                                      preferred_element_type=F32)

    def meta_init(near):
        def pair(j, carry):
            qt = qt_ref[j]
            for c in range(2):
                init_from_meta(j, c, _head_q(qt, c, fox), near)
            return carry
        lax.fori_loop(0, N_PAIRS, pair, 0)

    def run(diag):
        n_groups = N_PAIRS // PAIR_GROUP

        def scores(g):
            for u in range(PAIR_GROUP):
                j = g * PAIR_GROUP + u
                qt = qt_ref[j]
                for c in range(2):
                    s_sc[g % 2][2 * u + c] = tile_scores(j, c, _head_q(qt, c, fox), diag)

        def finish_group(g):
            for u in range(PAIR_GROUP):
                j = g * PAIR_GROUP + u
                ap = [tile_probs(j, c, s_sc[g % 2][2 * u + c]) for c in range(2)]
                for c in range(2):
                    tile_values(j, c, *ap[c])
                if diag:
                    if fox:
                        o = _fox_finish(acc_sc[0][j], acc_sc[1][j])
                    else:
                        o = _diff_finish(acc_sc[0][j], acc_sc[1][j],
                                         _diff_lambda(lam_ref, lambda_init),
                                         subln_ref[...], lambda_init)
                    ot_ref[j] = o.astype(ot_ref.dtype)

        scores(0)
        for g in range(n_groups):
            if g + 1 < n_groups:
                scores(g + 1)
            finish_group(g)

    @pl.when(ki == 0)
    def _():
        if fox:
            meta_init(False)
        else:
            @pl.when(qi == 0)
            def _():
                meta_init(True)

            @pl.when(qi > 0)
            def _():
                meta_init(False)

    @pl.when(ki == qi)
    def _():
        run(True)

    @pl.when(ki < qi)
    def _():
        run(False)


def _attn_meta_kernel(qmt_ref, k_ref, vmt_ref, *rest, fox, lambda_init):
    if fox:
        ckm_ref, ot_ref = rest
    else:
        dtab_ref, lam_ref, subln_ref, ot_ref = rest
    causal = _causal_block()[:N_META, :N_META]

    def pair(j, carry):
        qt = qmt_ref[0, j]
        km = k_ref[j]
        acc = []
        for c in range(2):
            mi = 2 * j + c
            qa = _head_q(qt, c, fox)
            if fox:
                s = (jnp.dot(_head_k(km, c), qa, preferred_element_type=F32)
                     - ckm_ref[0, mi] + causal)
            else:
                s = jnp.dot(km, qa, preferred_element_type=F32) + dtab_ref[mi, 0][:N_META, :N_META]
            m = jnp.max(s, axis=0, keepdims=True)
            p = jnp.exp2(s - m)
            acc.append(jnp.dot(_head_v(vmt_ref[0, j], c, fox), p.astype(BF16),
                               preferred_element_type=F32))
        if fox:
            o = _fox_finish(acc[0], acc[1])
        else:
            o = _diff_finish(acc[0], acc[1], _diff_lambda(lam_ref, lambda_init),
                             subln_ref[...][:, :N_META], lambda_init)
        ot_ref[0, j] = o.astype(ot_ref.dtype)
        return carry

    lax.fori_loop(0, N_PAIRS, pair, 0)


def _meta_columns(xt, bsz, rr):
    cols = xt[:, :, rr:rr + bsz * N_META]
    return cols.reshape(N_PAIRS, LANES, bsz, N_META).transpose(2, 0, 1, 3)


def _attention(qt, k, vt, bsz, seq, *, fox, lambda_init=0.0, extras=(), kaug=None):
    rr = bsz * seq
    r = qt.shape[2]
    tq = ATT_TILE
    nq = seq // tq
    qmap = np.concatenate([np.full(i + 1, i) for i in range(nq)]).astype(np.int32)
    kmap = np.concatenate([np.arange(i + 1) for i in range(nq)]).astype(np.int32)
    nsteps = int(qmap.shape[0])
    meta_blk = rr // N_META
    qmt = _meta_columns(qt, bsz, rr)
    vmt = _meta_columns(vt, bsz, rr)
    keys = kaug if fox else k
    nkh = keys.shape[0]

    tile_qt = pl.BlockSpec((N_PAIRS, LANES, tq), lambda b, t, qm, km: (0, 0, b * nq + qm[t]))
    tile_vt = pl.BlockSpec((N_PAIRS, LANES, tq), lambda b, t, qm, km: (0, 0, b * nq + km[t]))
    tile_k = pl.BlockSpec((nkh, tq, LANES), lambda b, t, qm, km: (0, b * nq + km[t], 0))
    meta_k = pl.BlockSpec((N_PAIRS, N_META, LANES), lambda b, t, qm, km: (0, meta_blk + b, 0))
    meta_vt = pl.BlockSpec((1, N_PAIRS, LANES, N_META), lambda b, t, qm, km: (b, 0, 0, 0))

    if fox:
        cmeta = extras[0]
        extra_specs = [pl.BlockSpec((1,) + cmeta.shape[1:], lambda b, t, qm, km: (b, 0, 0, 0))]
    else:
        extra_specs = [pl.BlockSpec(a.shape, functools.partial(lambda nd, b, t, qm, km: (0,) * nd,
                                                               a.ndim)) for a in extras]

    ot = pl.pallas_call(
        functools.partial(_attn_kernel, fox=fox, lambda_init=lambda_init),
        grid_spec=pltpu.PrefetchScalarGridSpec(
            num_scalar_prefetch=2, grid=(bsz, nsteps),
            in_specs=[tile_qt, tile_k, tile_vt, meta_k, meta_vt] + extra_specs,
            out_specs=tile_qt,
            scratch_shapes=[pltpu.VMEM((N_PAIRS, 1, tq), F32)] * 2
            + [pltpu.VMEM((N_PAIRS, LANES if fox else LANES + ONES_ROWS, tq), F32)] * 2
            + [pltpu.VMEM((2 * PAIR_GROUP, tq, tq), F32)] * 2),
        out_shape=jax.ShapeDtypeStruct((N_PAIRS, LANES, rr), BF16),
        compiler_params=pltpu.CompilerParams(
            dimension_semantics=("arbitrary", "arbitrary"), vmem_limit_bytes=VMEM_LIMIT),
        name="attn_fox" if fox else "attn_diff",
    )(jnp.asarray(qmap), jnp.asarray(kmap), qt, keys, vt, k, vmt, *extras)

    per_b = lambda b: (b, 0, 0, 0)
    if fox:
        m_extras = (extras[0],)
        m_specs = [pl.BlockSpec((1,) + extras[0].shape[1:], per_b)]
    else:
        m_extras = (extras[0], extras[2], extras[3])
        m_specs = [pl.BlockSpec(a.shape, functools.partial(lambda nd, b: (0,) * nd, a.ndim))
                   for a in m_extras]
    ot_meta = pl.pallas_call(
        functools.partial(_attn_meta_kernel, fox=fox, lambda_init=lambda_init),
        grid=(bsz,),
        in_specs=[pl.BlockSpec((1, N_PAIRS, LANES, N_META), per_b),
                  pl.BlockSpec((N_PAIRS, N_META, LANES), lambda b: (0, meta_blk + b, 0)),
                  pl.BlockSpec((1, N_PAIRS, LANES, N_META), per_b)] + m_specs,
        out_specs=pl.BlockSpec((1, N_PAIRS, LANES, N_META), per_b),
        out_shape=jax.ShapeDtypeStruct((bsz, N_PAIRS, LANES, N_META), BF16),
        compiler_params=pltpu.CompilerParams(dimension_semantics=("arbitrary",)),
        name="attn_fox_meta" if fox else "attn_diff_meta",
    )(qmt, k, vmt, *m_extras)
    tail = ot_meta.transpose(1, 2, 0, 3).reshape(N_PAIRS, LANES, bsz * N_META)
    tail = jnp.pad(tail, ((0, 0), (0, 0), (0, r - rr - bsz * N_META)))
    return ot, tail


def _layer_norm(y, g, b):
    mu = jnp.mean(y, axis=1, keepdims=True)
    yc = y - mu
    var = jnp.mean(yc * yc, axis=1, keepdims=True)
    return yc * lax.rsqrt(var + LN_EPS) * g + b


def _mix_router_kernel(ot_ref, otm_ref, h_ref, wo_ref, g_ref, b_ref, wr_ref, br_ref,
                       h1_ref, route_ref, route_t_ref, counts_ref, carry_ref):
    i = pl.program_id(0)
    tm = h_ref.shape[0]

    @pl.when(i == 0)
    def _():
        carry_ref[...] = jnp.zeros_like(carry_ref)

    is_tail = i == pl.num_programs(0) - 1
    ot = jnp.concatenate([jnp.where(is_tail, otm_ref[j], ot_ref[j]) for j in range(N_PAIRS)],
                         axis=0)
    mix = lax.dot_general(ot, wo_ref[...], _TN, preferred_element_type=F32)
    h1 = _layer_norm(DN_ALPHA * h_ref[...] + mix, g_ref[...], b_ref[...])
    h1_ref[...] = h1

    logits = jnp.dot(h1.astype(BF16), wr_ref[...], preferred_element_type=F32) + br_ref[...]
    lane = lax.broadcasted_iota(jnp.int32, logits.shape, 1)
    lane_f = lane.astype(F32)
    big = float(LANES)
    g_mask = lane < N_GROUPS
    lg = jnp.where(g_mask, logits, NEG_INF)
    gmax = jnp.max(lg, axis=1, keepdims=True)
    gsel = jnp.min(jnp.where(lg == gmax, lane_f, big), axis=1, keepdims=True)
    p_group = 1.0 / jnp.sum(jnp.where(g_mask, jnp.exp(lg - gmax), 0.0), axis=1, keepdims=True)
    e_lo = N_GROUPS + EXPERTS_PER_GROUP * gsel
    e_mask = jnp.logical_and(lane_f >= e_lo, lane_f < e_lo + EXPERTS_PER_GROUP)
    le = jnp.where(e_mask, logits, NEG_INF)
    v1 = jnp.max(le, axis=1, keepdims=True)
    i1 = jnp.min(jnp.where(jnp.logical_and(le == v1, e_mask), lane_f, big),
                 axis=1, keepdims=True)
    e_mask2 = jnp.logical_and(e_mask, lane_f != i1)
    le2 = jnp.where(e_mask2, logits, NEG_INF)
    v2 = jnp.max(le2, axis=1, keepdims=True)
    i2 = jnp.min(jnp.where(jnp.logical_and(le2 == v2, e_mask2), lane_f, big),
                 axis=1, keepdims=True)
    ex = jnp.exp(v2 - v1)
    gate0 = p_group * (1.0 / (1.0 + ex))
    gate1 = p_group * (ex / (1.0 + ex))
    e0 = i1 - N_GROUPS
    e1 = i2 - N_GROUPS

    onehot = jnp.logical_or(lane_f == e0, lane_f == e1)
    oh_bf = jnp.where(onehot, 1.0, 0.0).astype(BF16)
    ltri = jnp.where(lax.broadcasted_iota(jnp.int32, (tm, tm), 1)
                     < lax.broadcasted_iota(jnp.int32, (tm, tm), 0), 1.0, 0.0).astype(BF16)
    before = jnp.dot(ltri, oh_bf, preferred_element_type=F32) + carry_ref[...]
    r0 = jnp.sum(jnp.where(lane_f == e0, before, 0.0), axis=1, keepdims=True)
    r1 = jnp.sum(jnp.where(lane_f == e1, before, 0.0), axis=1, keepdims=True)
    carry = carry_ref[...] + jnp.sum(jnp.where(onehot, 1.0, 0.0), axis=0, keepdims=True)
    carry_ref[...] = carry
    counts_ref[...] = carry

    route = jnp.zeros_like(logits)
    for col, val in enumerate((e0, e1, r0, r1, gate0, gate1)):
        route = jnp.where(lane == col, val, route)
    route_ref[...] = route
    route_t_ref[...] = route.T[:route_t_ref.shape[0], :]


def _mix_and_route(ot, ot_tail, h, wo_bf, ln_g, ln_b, wr_bf, br):
    r, d = h.shape
    tm = ROW_TILE
    assert ot_tail.shape[2] == tm and ot.shape[2] == r - tm
    last_real = ot.shape[2] // tm - 1
    row = lambda i: (i, 0)
    const = lambda i: (0, 0)
    return pl.pallas_call(
        _mix_router_kernel, grid=(r // tm,),
        in_specs=[pl.BlockSpec((N_PAIRS, LANES, tm), lambda i: (0, 0, jnp.minimum(i, last_real))),
                  pl.BlockSpec((N_PAIRS, LANES, tm), lambda i: (0, 0, 0)),
                  pl.BlockSpec((tm, d), row),
                  pl.BlockSpec((d, d), const),
                  pl.BlockSpec((1, d), const), pl.BlockSpec((1, d), const),
                  pl.BlockSpec((d, LANES), const), pl.BlockSpec((1, LANES), const)],
        out_specs=[pl.BlockSpec((tm, d), row), pl.BlockSpec((tm, LANES), row),
                   pl.BlockSpec((8, tm), lambda i: (0, i)), pl.BlockSpec((1, LANES), const)],
        out_shape=[jax.ShapeDtypeStruct((r, d), F32), jax.ShapeDtypeStruct((r, LANES), F32),
                   jax.ShapeDtypeStruct((8, r), F32), jax.ShapeDtypeStruct((1, LANES), F32)],
        scratch_shapes=[pltpu.VMEM((1, LANES), F32)],
        compiler_params=pltpu.CompilerParams(dimension_semantics=("arbitrary",),
                                             vmem_limit_bytes=VMEM_LIMIT),
        name="mix_router")(ot, ot_tail, h, wo_bf, ln_g, ln_b, wr_bf, br)


def _row_copy(src_ref, src_row, dst_ref, dst_row, sem):
    return pltpu.make_async_copy(src_ref.at[pl.ds(src_row, 1), :],
                                 dst_ref.at[pl.ds(dst_row, 1), :], sem)


def _fetch_slots(slots_ref, slot_smem, idx_sem):
    i = pl.program_id(0)
    n = slot_smem.shape[0] // 2
    cur = lax.rem(i, 2)

    def copy(step, buf):
        dst = slot_smem.at[pl.ds(pl.multiple_of(buf * n, n), n)]
        return pltpu.make_async_copy(slots_ref.at[step], dst, idx_sem.at[buf])

    @pl.when(i == 0)
    def _():
        copy(0, 0).start()

    @pl.when(i + 1 < pl.num_programs(0))
    def _():
        copy(i + 1, 1 - cur).start()

    copy(i, cur).wait()
    return cur * n


def _dispatch_kernel(slots_ref, h_ref, xs_in_ref, xs_ref, slot_smem, sem, idx_sem):
    del xs_in_ref
    tm = h_ref.shape[0]
    base = _fetch_slots(slots_ref, slot_smem, idx_sem)

    def start(t, carry):
        for c in range(2):
            _row_copy(h_ref, t, xs_ref, slot_smem[base + c * tm + t], sem).start(priority=c)
        return carry

    def wait(t, carry):
        for c in range(2):
            _row_copy(h_ref, t, xs_ref, slot_smem[base + c * tm + t], sem).wait()
        return carry

    lax.fori_loop(0, tm, start, 0, unroll=DMA_UNROLL)
    lax.fori_loop(0, tm, wait, 0, unroll=DMA_UNROLL)


def _dispatch(h1, slots, n_rows):
    r, d = h1.shape
    tm = ROW_TILE
    xs0 = jnp.zeros((n_rows, d), F32)
    any_spec = pl.BlockSpec(memory_space=pl.ANY)
    return pl.pallas_call(
        _dispatch_kernel, grid=(r // tm,),
        in_specs=[any_spec, pl.BlockSpec((tm, d), lambda i: (i, 0)), any_spec],
        out_specs=any_spec,
        out_shape=jax.ShapeDtypeStruct((n_rows, d), F32),
        scratch_shapes=[pltpu.SMEM((4 * tm,), jnp.int32), pltpu.SemaphoreType.DMA,
                        pltpu.SemaphoreType.DMA((2,))],
        input_output_aliases={2: 0},
        compiler_params=pltpu.CompilerParams(dimension_semantics=("arbitrary",)),
        name="moe_dispatch")(slots, h1, xs0)


def _expert_kernel(tile_e_ref, n_used_ref, x_ref, wg_ref, wu_ref, wd_ref, y_ref):
    del tile_e_ref

    @pl.when(pl.program_id(0) < n_used_ref[0])
    def _():
        xb = x_ref[...].astype(BF16)
        g = jnp.dot(xb, wg_ref[0, 0].astype(BF16), preferred_element_type=F32)
        u = jnp.dot(xb, wu_ref[0, 0].astype(BF16), preferred_element_type=F32)
        a = (g * jax.nn.sigmoid(g) * u).astype(BF16)
        y_ref[...] = jnp.dot(a, wd_ref[0, 0].astype(BF16), preferred_element_type=F32)

    @pl.when(pl.program_id(0) >= n_used_ref[0])
    def _():
        y_ref[...] = jnp.zeros_like(y_ref)


def _experts(xs, tile_e, n_used, layer, w_gate, w_up, w_down):
    p, d = xs.shape
    te = EXP_TILE
    de = w_gate.shape[3]
    row = lambda i, te_ref, nu_ref: (jnp.minimum(i, nu_ref[0] - 1), 0)
    wsel = lambda i, te_ref, nu_ref: (layer, te_ref[i], 0, 0)
    return pl.pallas_call(
        _expert_kernel,
        grid_spec=pltpu.PrefetchScalarGridSpec(
            num_scalar_prefetch=2, grid=(p // te,),
            in_specs=[pl.BlockSpec((te, d), row),
                      pl.BlockSpec((1, 1, d, de), wsel), pl.BlockSpec((1, 1, d, de), wsel),
                      pl.BlockSpec((1, 1, de, d), wsel)],
            out_specs=pl.BlockSpec((te, d), lambda i, te_ref, nu_ref: (i, 0))),
        out_shape=jax.ShapeDtypeStruct((p, d), F32),
        compiler_params=pltpu.CompilerParams(dimension_semantics=("arbitrary",),
                                             vmem_limit_bytes=VMEM_LIMIT),
        name="moe_experts")(tile_e, n_used, xs, w_gate, w_up, w_down)


def _combine_kernel(slots_ref, ys_ref, route_ref, h_ref, g_ref, b_ref, out_ref,
                    slot_smem, y0_ref, y1_ref, sem, idx_sem):
    tm = h_ref.shape[0]
    idx_copy = pltpu.make_async_copy(slots_ref.at[pl.program_id(0)], slot_smem, idx_sem)
    idx_copy.start()
    idx_copy.wait()
    base = 0
    bufs = (y0_ref, y1_ref)

    def start(t, carry):
        for c in range(2):
            _row_copy(ys_ref, slot_smem[base + c * tm + t], bufs[c], t, sem).start(priority=c)
        return carry

    def wait(t, carry):
        for c in range(2):
            _row_copy(ys_ref, slot_smem[base + c * tm + t], bufs[c], t, sem).wait()
        return carry

    lax.fori_loop(0, tm, start, 0, unroll=DMA_UNROLL)
    lax.fori_loop(0, tm, wait, 0, unroll=DMA_UNROLL)
    route = route_ref[...]
    ffn = route[:, 4:5] * y0_ref[...] + route[:, 5:6] * y1_ref[...]
    out_ref[...] = _layer_norm(DN_ALPHA * h_ref[...] + ffn, g_ref[...], b_ref[...])


def _combine(ys, slots, route, h1, ln_g, ln_b, n_out_rows):
    r, d = h1.shape
    tm = ROW_TILE
    row = lambda i: (i, 0)
    const = lambda i: (0, 0)
    any_spec = pl.BlockSpec(memory_space=pl.ANY)
    return pl.pallas_call(
        _combine_kernel, grid=(n_out_rows // tm,),
        in_specs=[any_spec, any_spec, pl.BlockSpec((tm, LANES), row),
                  pl.BlockSpec((tm, d), row), pl.BlockSpec((1, d), const),
                  pl.BlockSpec((1, d), const)],
        out_specs=pl.BlockSpec((tm, d), row),
        out_shape=jax.ShapeDtypeStruct((n_out_rows, d), F32),
        scratch_shapes=[pltpu.SMEM((2 * tm,), jnp.int32), pltpu.VMEM((tm, d), F32),
                        pltpu.VMEM((tm, d), F32), pltpu.SemaphoreType.DMA,
                        pltpu.SemaphoreType.DMA],
        compiler_params=pltpu.CompilerParams(dimension_semantics=("arbitrary",),
                                             vmem_limit_bytes=VMEM_LIMIT),
        name="moe_combine")(slots, ys, route, h1, ln_g, ln_b)


def _moe(h1, route, route_t, counts, layer, w_gate, w_up, w_down, ln_g, ln_b, n_out_rows):
    r, _ = h1.shape
    tm, te = ROW_TILE, EXP_TILE
    n_tiles = (2 * r) // te + N_EXPERTS
    cnt = counts[0, :N_EXPERTS].astype(jnp.int32)
    seg_tiles = (cnt + te - 1) // te
    seg_end = jnp.cumsum(seg_tiles)
    seg_start = (seg_end - seg_tiles) * te
    n_used = seg_end[-1:]
    tile_ids = jnp.arange(n_tiles, dtype=jnp.int32)
    tile_e = jnp.sum(tile_ids[:, None] >= seg_end[None, :], axis=1)
    last_e = jnp.sum(n_used - 1 >= seg_end)
    tile_e = jnp.minimum(tile_e, last_e).astype(jnp.int32)
    e01 = route_t[0:2].astype(jnp.int32)
    first = jnp.sum(jnp.where(e01[..., None] == jnp.arange(N_EXPERTS), seg_start, 0), axis=-1)
    slot = first + route_t[2:4].astype(jnp.int32)
    slots = slot.reshape(2, r // tm, tm).transpose(1, 0, 2).reshape(r // tm, 2 * tm)

    xs = _dispatch(h1, slots, n_tiles * te)
    ys = _experts(xs, tile_e, n_used.astype(jnp.int32), layer, w_gate, w_up, w_down)
    return _combine(ys, slots, route, h1, ln_g, ln_b, n_out_rows)


def _diff_tables(rel_bias):
    bkt = _bucket_table(2 * LANES + N_META)
    rel = (rel_bias - rel_bias[N_BUCKETS - 1:N_BUCKETS]).T * LOG2E

    def lookup(dist):
        onehot = (bkt[dist][None] == np.arange(N_BUCKETS)[:, None, None]).astype(np.float32)
        return jnp.einsum("mb,bkq->mkq", rel, jnp.asarray(onehot), precision=lax.Precision.HIGHEST)

    key = np.arange(LANES)[:, None]
    qry = np.arange(LANES)[None, :]
    d0 = jnp.where(jnp.asarray(qry >= key), lookup(np.maximum(qry - key, 0)), NEG_INF)
    d1 = lookup(LANES + qry - key)
    dtab = jnp.stack([d0, d1], axis=1)
    km = np.arange(N_META)[:, None]
    mtab = lookup(N_META + qry - km)
    return dtab.astype(F32), mtab.astype(F32)


def kernel(x, meta_tokens, rel_bias, diff_w_qkv, diff_lambda, diff_subln, diff_w_o,
           fox_w_in, fox_b_f, fox_w_o, ln_mix_g, ln_mix_b, ln_ffn_g, ln_ffn_b,
           router_group_w, router_group_b, router_expert_w, router_expert_b,
           expert_w_gate, expert_w_up, expert_w_down):
    bsz, seq, d = x.shape
    rr = bsz * seq
    r = rr + ROW_TILE
    assert seq % ATT_TILE == 0 and bsz * N_META <= ROW_TILE and d == N_PAIRS * LANES
    n_pad = r - rr - bsz * N_META
    h = jnp.concatenate([x.reshape(rr, d),
                         jnp.tile(meta_tokens.astype(x.dtype), (bsz, 1)),
                         jnp.zeros((n_pad, d), x.dtype)], axis=0)

    for i in range(DEPTH):
        j = i // 2
        if i % 2 == 0:
            lam0 = _lambda_init(i)
            qt, k, vt = _project(h, diff_w_qkv[j])
            dtab, mtab = _diff_tables(rel_bias)
            subln = jnp.broadcast_to(diff_subln[j].astype(F32)[:, None], (2 * HEAD_DIM, ATT_TILE))
            ot, ot_tail = _attention(qt, k, vt, bsz, seq, fox=False, lambda_init=lam0,
                                     extras=(dtab, mtab, diff_lambda[j].astype(F32), subln))
            w_o = diff_w_o[j]
        else:
            w_in = fox_w_in[j]
            qt, k, vt, lft = _project(h, w_in[:, :3 * d], w_in[:, 3 * d:].T.astype(BF16),
                                      fox_b_f[j].astype(F32)[:, None])
            cmeta, kaug = _fox_cumsum(lft, k, bsz, seq)
            ot, ot_tail = _attention(qt, k, vt, bsz, seq, fox=True,
                                     extras=(cmeta[..., None],), kaug=kaug)
            w_o = fox_w_o[j]

        n_router = N_GROUPS + N_EXPERTS
        wr = jnp.concatenate([router_group_w[i], router_expert_w[i].reshape(d, N_EXPERTS)], axis=1)
        wr = jnp.pad(wr, ((0, 0), (0, LANES - n_router))).astype(BF16)
        br = jnp.concatenate([router_group_b[i], router_expert_b[i].reshape(N_EXPERTS)])
        br = jnp.pad(br, (0, LANES - n_router)).astype(F32)[None, :]
        h1, route, route_t, counts = _mix_and_route(
            ot, ot_tail, h, w_o.astype(BF16), ln_mix_g[i][None, :], ln_mix_b[i][None, :], wr, br)
        n_out = rr if i == DEPTH - 1 else r
        h = _moe(h1, route, route_t, counts, i, expert_w_gate, expert_w_up, expert_w_down,
                 ln_ffn_g[i][None, :], ln_ffn_b[i][None, :], n_out)

    return h.reshape(bsz, seq, d)
```
